```python
import math
import jax, jax.numpy as jnp
from jax import lax
import numpy as np

D_MODEL = 1024
BATCH = 8
SEQ = 8192
DEPTH = 2

CHUNK = 128
GMLP_WIDTH = D_MODEL // 2
SGU_GROUPS = 4
SGU_GROUP_WIDTH = GMLP_WIDTH // SGU_GROUPS
DIFF_HEADS = 4
DIFF_HEAD_DIM = D_MODEL // 16
DIFF_QK_WIDTH = 2 * DIFF_HEADS * DIFF_HEAD_DIM
DIFF_V_WIDTH = DIFF_HEADS * 2 * DIFF_HEAD_DIM
ROPE_DIM = DIFF_HEAD_DIM // 4
ROPE_THETA = 500000.0
Q_BLOCK = 128
CONV_WIDTH = D_MODEL // 2
CONV_K = 3
N_BRANCHES = 3
BRANCH_WIDTH = 512
IN_WIDTH = 2 * GMLP_WIDTH + 2 * DIFF_QK_WIDTH + DIFF_V_WIDTH + 3 * CONV_WIDTH + N_BRANCHES * D_MODEL
D_FF = ((8 * D_MODEL // 3 + 255) // 256) * 256
ALPHA = (2 * DEPTH) ** 0.25
BETA = (8 * DEPTH) ** -0.25
EPS = 1e-5
MAX_POS_OFFSET = 4096

kernel_name = "hybrid_gmlp_diffattn_shortconv_deepnorm"


def _layer_norm(x, g, b):
    xf = x.astype(jnp.float32)
    mu = jnp.mean(xf, axis=-1, keepdims=True)
    var = jnp.mean(jnp.square(xf - mu), axis=-1, keepdims=True)
    return ((xf - mu) * lax.rsqrt(var + EPS) * g.astype(jnp.float32) + b.astype(jnp.float32)).astype(x.dtype)


def _rms_norm(x, g):
    xf = x.astype(jnp.float32)
    ms = jnp.mean(jnp.square(xf), axis=-1, keepdims=True)
    return (xf * lax.rsqrt(ms + EPS) * g.astype(jnp.float32)).astype(x.dtype)


def _rope_tables(positions):
    inv_freq = ROPE_THETA ** (-jnp.arange(0, ROPE_DIM, 2, dtype=jnp.float32) / ROPE_DIM)
    ang = positions.astype(jnp.float32)[..., None] * inv_freq
    return jnp.cos(ang)[:, :, None, :], jnp.sin(ang)[:, :, None, :]


def _partial_rope(t, cos, sin):
    tf = t.astype(jnp.float32)
    half = ROPE_DIM // 2
    r1 = tf[..., :half]
    r2 = tf[..., half:ROPE_DIM]
    out = jnp.concatenate([r1 * cos - r2 * sin, r2 * cos + r1 * sin, tf[..., ROPE_DIM:]], axis=-1)
    return out.astype(t.dtype)


def _chunked_sgu(z, w_s, b_s, ln_g, ln_b):
    z = jax.nn.gelu(z, approximate=False)
    u, v = jnp.split(z, 2, axis=-1)
    v = _layer_norm(v, ln_g, ln_b)
    bsz, s_len, _ = v.shape
    v = v.reshape(bsz, s_len // CHUNK, CHUNK, SGU_GROUPS, SGU_GROUP_WIDTH)
    causal = jnp.tril(jnp.ones((CHUNK, CHUNK), dtype=bool))
    w = jnp.where(causal[None], w_s, jnp.zeros((), w_s.dtype))
    s = jnp.einsum('gij,bcjgd->bcigd', w, v) + b_s.T[:, :, None]
    return u * s.reshape(bsz, s_len, GMLP_WIDTH)


def _diff_attention(q, k, v, cos, sin, lq1, lk1, lq2, lk2, subln_g, lambda_init):
    bsz, s_len, _ = q.shape
    q = _partial_rope(q.reshape(bsz, s_len, 2 * DIFF_HEADS, DIFF_HEAD_DIM), cos, sin)
    k = _partial_rope(k.reshape(bsz, s_len, 2 * DIFF_HEADS, DIFF_HEAD_DIM), cos, sin)
    v = v.reshape(bsz, s_len, DIFF_HEADS, 2 * DIFF_HEAD_DIM)
    lam = (jnp.exp(jnp.sum(lq1.astype(jnp.float32) * lk1.astype(jnp.float32)))
           - jnp.exp(jnp.sum(lq2.astype(jnp.float32) * lk2.astype(jnp.float32)))
           + lambda_init)
    n_blocks = s_len // Q_BLOCK
    q_blocks = q.reshape(bsz, n_blocks, Q_BLOCK, 2 * DIFF_HEADS, DIFF_HEAD_DIM).transpose(1, 0, 2, 3, 4)
    starts = jnp.arange(n_blocks, dtype=jnp.int32) * Q_BLOCK
    k_pos = jnp.arange(s_len, dtype=jnp.int32)
    scale = DIFF_HEAD_DIM ** -0.5

    def one_block(args):
        qb, start = args
        sc = jnp.einsum('bqhd,bkhd->bhqk', qb, k).astype(jnp.float32) * scale
        mask = (start + jnp.arange(Q_BLOCK, dtype=jnp.int32))[:, None] >= k_pos[None, :]
        sc = jnp.where(mask[None, None], sc, -jnp.inf)
        p = jax.nn.softmax(sc, axis=-1).reshape(bsz, DIFF_HEADS, 2, Q_BLOCK, s_len)
        a = p[:, :, 0] - lam * p[:, :, 1]
        return jnp.einsum('bhqk,bkhe->bqhe', a.astype(v.dtype), v)

    o = lax.map(one_block, (q_blocks, starts))
    o = o.transpose(1, 0, 2, 3, 4).reshape(bsz, s_len, DIFF_HEADS, 2 * DIFF_HEAD_DIM)
    o = _rms_norm(o, subln_g) * (1.0 - lambda_init)
    return o.reshape(bsz, s_len, DIFF_V_WIDTH)


def _short_conv(z, conv_w):
    bg, cg, xc = jnp.split(z, 3, axis=-1)
    h = cg * xc
    s_len = h.shape[1]
    hp = jnp.pad(h, ((0, 0), (CONV_K - 1, 0), (0, 0)))
    conv = sum(hp[:, j:j + s_len, :] * conv_w[:, j] for j in range(CONV_K))
    return bg * conv


def _mixer(x, cos, sin, w_in, w_sgu, b_sgu, sgu_ln_g, sgu_ln_b, lq1, lk1, lq2, lk2,
           subln_g, conv_w, w_branch, w_o, lambda_init):
    bsz, s_len, _ = x.shape
    z = x @ w_in
    o1 = 2 * GMLP_WIDTH
    o2 = o1 + DIFF_QK_WIDTH
    o3 = o2 + DIFF_QK_WIDTH
    o4 = o3 + DIFF_V_WIDTH
    o5 = o4 + 3 * CONV_WIDTH
    za, zq, zk, zv, zc, zg = jnp.split(z, [o1, o2, o3, o4, o5], axis=-1)
    y_a = _chunked_sgu(za, w_sgu, b_sgu, sgu_ln_g, sgu_ln_b)
    y_b = _diff_attention(zq, zk, zv, cos, sin, lq1, lk1, lq2, lk2, subln_g, lambda_init)
    y_c = _short_conv(zc, conv_w)
    branches = jnp.stack([y_a, y_b, y_c], axis=2)
    branches = jnp.einsum('bsnc,ncd->bsnd', branches, w_branch)
    gates = jax.nn.sigmoid(zg.reshape(bsz, s_len, N_BRANCHES, D_MODEL))
    merged = jnp.sum(gates * branches, axis=2)
    return merged @ w_o


def _swiglu(x, w_gate_up, w_down):
    gate, up = jnp.split(x @ w_gate_up, 2, axis=-1)
    return (jax.nn.silu(gate) * up) @ w_down


def setup_inputs(seed: int = 0) -> dict:
    key = jax.random.key(seed)
    ks = jax.random.split(key, 24)
    f32 = jnp.float32

    def nrm(k, shape, fan_in, scale=1.0):
        return jax.random.normal(k, shape, f32) * (scale * fan_in ** -0.5)

    def gain(k, shape):
        return 1.0 + 0.01 * jax.random.normal(k, shape, f32)

    def small(k, shape, s=0.01):
        return s * jax.random.normal(k, shape, f32)

    x = jax.random.normal(ks[0], (BATCH, SEQ, D_MODEL), f32)
    positions = (jax.random.randint(ks[1], (BATCH, 1), 0, MAX_POS_OFFSET, dtype=jnp.int32)
                 + jnp.arange(SEQ, dtype=jnp.int32)[None, :])
    return {
        "x": x,
        "positions": positions,
        "w_in": nrm(ks[2], (DEPTH, D_MODEL, IN_WIDTH), D_MODEL),
        "w_sgu": nrm(ks[3], (DEPTH, SGU_GROUPS, CHUNK, CHUNK), CHUNK),
        "b_sgu": gain(ks[4], (DEPTH, SGU_GROUPS, CHUNK)),
        "sgu_ln_g": gain(ks[5], (DEPTH, GMLP_WIDTH)),
        "sgu_ln_b": small(ks[6], (DEPTH, GMLP_WIDTH)),
        "lambda_q1": small(ks[7], (DEPTH, DIFF_HEAD_DIM), 0.1),
        "lambda_k1": small(ks[8], (DEPTH, DIFF_HEAD_DIM), 0.1),
        "lambda_q2": small(ks[9], (DEPTH, DIFF_HEAD_DIM), 0.1),
        "lambda_k2": small(ks[10], (DEPTH, DIFF_HEAD_DIM), 0.1),
        "subln_g": gain(ks[11], (DEPTH, 2 * DIFF_HEAD_DIM)),
        "conv_w": nrm(ks[12], (DEPTH, CONV_WIDTH, CONV_K), CONV_K),
        "w_branch": nrm(ks[13], (DEPTH, N_BRANCHES, BRANCH_WIDTH, D_MODEL), BRANCH_WIDTH),
        "w_o": nrm(ks[14], (DEPTH, D_MODEL, D_MODEL), D_MODEL, BETA),
        "ln1_g": gain(ks[15], (DEPTH, D_MODEL)),
        "ln1_b": small(ks[16], (DEPTH, D_MODEL)),
        "w_gate_up": nrm(ks[17], (DEPTH, D_MODEL, 2 * D_FF), D_MODEL),
        "w_down": nrm(ks[18], (DEPTH, D_FF, D_MODEL), D_FF, BETA),
        "ln2_g": gain(ks[19], (DEPTH, D_MODEL)),
        "ln2_b": small(ks[20], (DEPTH, D_MODEL)),
    }


def reference(x, positions, w_in, w_sgu, b_sgu, sgu_ln_g, sgu_ln_b, lambda_q1, lambda_k1,
              lambda_q2, lambda_k2, subln_g, conv_w, w_branch, w_o, ln1_g, ln1_b,
              w_gate_up, w_down, ln2_g, ln2_b):
    cos, sin = _rope_tables(positions)
    for l in range(DEPTH):
        lambda_init = 0.8 - 0.6 * math.exp(-0.3 * l)
        mix = _mixer(x, cos, sin, w_in[l], w_sgu[l], b_sgu[l], sgu_ln_g[l], sgu_ln_b[l],
                     lambda_q1[l], lambda_k1[l], lambda_q2[l], lambda_k2[l], subln_g[l],
                     conv_w[l], w_branch[l], w_o[l], lambda_init)
        x = _layer_norm(ALPHA * x + mix, ln1_g[l], ln1_b[l])
        x = _layer_norm(ALPHA * x + _swiglu(x, w_gate_up[l], w_down[l]), ln2_g[l], ln2_b[l])
    return x
```

```python
import functools
import math

import jax
import jax.numpy as jnp
from jax import lax
from jax.experimental import pallas as pl
from jax.experimental.pallas import tpu as pltpu

F32 = jnp.float32
BF16 = jnp.bfloat16

CHUNK = 128
SGU_GROUPS = 4
DIFF_HEADS = 4
HEAD_DIM = 64
ROPE_DIM = HEAD_DIM // 4
ROPE_HALF = ROPE_DIM // 2
ROPE_THETA = 500000.0
CONV_K = 3
EPS = 1e-5
LANES = 128
NEG_BIG = -1e30

PROJ_TILE = 512
ATTN_BLOCK = 256
FFN_TILE = 512
VMEM_LIMIT = 56 * 1024 * 1024


def _dot(a, b):
    return jnp.dot(a, b, preferred_element_type=F32)


def _layer_norm(x, g, b):
    mu = jnp.mean(x, axis=-1, keepdims=True)
    xc = x - mu
    var = jnp.mean(xc * xc, axis=-1, keepdims=True)
    return xc * lax.rsqrt(var + EPS) * g + b


def _resident(shape):
    nd = len(shape)
    return pl.BlockSpec(shape, lambda *_: (0,) * nd, pipeline_mode=pl.Buffered(1))


def _in_proj_body(x_ref, rc_ref, rs_ref, w_in_ref, wsgu_ref, bsgu_ref, lng_ref, lnb_ref,
                  convw_ref, wb0_ref, wb2_ref,
                  qT_ref, k_ref, vT_ref, part_ref, gb_ref, carry_ref, *, d_model):
    tm = x_ref.shape[0]
    half = d_model // 2
    t = pl.program_id(1)
    xb = x_ref[...].astype(BF16)

    def proj(lo, hi):
        return _dot(xb, w_in_ref[:, lo:hi])

    za = proj(0, d_model)
    za = 0.5 * za * (1.0 + lax.erf(za * (1.0 / math.sqrt(2.0))))
    u = za[:, :half]
    v = _layer_norm(za[:, half:], lng_ref[...], lnb_ref[...]).astype(BF16)
    gw = half // SGU_GROUPS
    rows = []
    for c in range(tm // CHUNK):
        cols = [_dot(wsgu_ref[g], v[c * CHUNK:(c + 1) * CHUNK, g * gw:(g + 1) * gw])
                for g in range(SGU_GROUPS)]
        rows.append(jnp.concatenate(cols, axis=1) + bsgu_ref[...])
    y_a = (u * jnp.concatenate(rows, axis=0)).astype(BF16)

    o1 = d_model
    qk_w = 2 * DIFF_HEADS * HEAD_DIM
    v_w = DIFF_HEADS * 2 * HEAD_DIM
    rc = jnp.concatenate([rc_ref[...]] * (qk_w // LANES), axis=1)
    rs = jnp.concatenate([rs_ref[...]] * (qk_w // LANES), axis=1)
    lane = lax.broadcasted_iota(jnp.int32, (tm, qk_w), 1)
    first = (lane % HEAD_DIM) < ROPE_HALF

    def rope(z):
        partner = jnp.where(first, pltpu.roll(z, qk_w - ROPE_HALF, 1), pltpu.roll(z, ROPE_HALF, 1))
        return z * rc + partner * rs

    q = rope(proj(o1, o1 + qk_w)) * (HEAD_DIM ** -0.5)
    qT_ref[...] = q.T.astype(BF16)
    k_ref[...] = rope(proj(o1 + qk_w, o1 + 2 * qk_w)).astype(BF16)
    o3 = o1 + 2 * qk_w
    vT_ref[...] = proj(o3, o3 + v_w).T.astype(BF16)

    o4 = o3 + v_w
    zc = proj(o4, o4 + 3 * half)
    h = zc[:, half:2 * half] * zc[:, 2 * half:]

    @pl.when(t == 0)
    def _():
        carry_ref[...] = jnp.zeros_like(carry_ref)

    prev = carry_ref[...]
    row = lax.broadcasted_iota(jnp.int32, (tm, half), 0)
    h1 = jnp.where(row == 0, prev[7:8, :], pltpu.roll(h, 1, 0))
    h2 = jnp.where(row == 0, prev[6:7, :], jnp.where(row == 1, prev[7:8, :], pltpu.roll(h, 2, 0)))
    carry_ref[...] = h[tm - 8:, :]
    conv = h2 * convw_ref[0:1, :] + h1 * convw_ref[1:2, :] + h * convw_ref[2:3, :]
    y_c = (zc[:, :half] * conv).astype(BF16)

    o5 = o4 + 3 * half
    g_a = jax.nn.sigmoid(proj(o5, o5 + d_model))
    g_c = jax.nn.sigmoid(proj(o5 + 2 * d_model, o5 + 3 * d_model))
    part = g_a * _dot(y_a, wb0_ref[...])
    part = part + g_c * _dot(y_c, wb2_ref[...])
    part_ref[...] = part
    gb_ref[...] = jax.nn.sigmoid(proj(o5 + d_model, o5 + 2 * d_model)).astype(BF16)


def _in_proj_call(x, rope_c, rope_s, w_in, w_sgu, b_sgu, ln_g, ln_b, conv_w, wb0, wb2):
    bsz, s_len, d_model = x.shape
    tm = min(PROJ_TILE, s_len)
    half = d_model // 2
    qk_w = 2 * DIFF_HEADS * HEAD_DIM
    v_w = DIFF_HEADS * 2 * HEAD_DIM
    grid = (bsz, s_len // tm)
    tok = lambda w: pl.BlockSpec((None, tm, w), lambda b, t: (b, t, 0))
    tokT = lambda w: pl.BlockSpec((None, w, tm), lambda b, t: (b, 0, t))
    return pl.pallas_call(
        functools.partial(_in_proj_body, d_model=d_model),
        grid=grid,
        in_specs=[tok(d_model), tok(LANES), tok(LANES),
                  _resident(w_in.shape), _resident(w_sgu.shape), _resident(b_sgu.shape),
                  _resident(ln_g.shape), _resident(ln_b.shape), _resident(conv_w.shape),
                  _resident(wb0.shape), _resident(wb2.shape)],
        out_specs=[tokT(qk_w), tok(qk_w), tokT(v_w), tok(d_model), tok(d_model)],
        out_shape=[jax.ShapeDtypeStruct((bsz, qk_w, s_len), BF16),
                   jax.ShapeDtypeStruct((bsz, s_len, qk_w), BF16),
                   jax.ShapeDtypeStruct((bsz, v_w, s_len), BF16),
                   jax.ShapeDtypeStruct((bsz, s_len, d_model), F32),
                   jax.ShapeDtypeStruct((bsz, s_len, d_model), BF16)],
        scratch_shapes=[pltpu.VMEM((8, half), F32)],
        compiler_params=pltpu.CompilerParams(
            dimension_semantics=("arbitrary", "arbitrary"), vmem_limit_bytes=VMEM_LIMIT),
        name="in_proj",
    )(x, rope_c, rope_s, w_in, w_sgu, b_sgu, ln_g, ln_b, conv_w, wb0, wb2)


def _attn_body(lam_ref, qT_ref, k_ref, vT_ref, part_ref, gb_ref, x_ref, subg_ref, wb1_ref,
               wo_ref, g1_ref, b1_ref, o_ref, *, alpha, out_scale):
    blk = qT_ref.shape[1]
    i = pl.program_id(1)
    lam = lam_ref[0, 0]
    pair_w = 2 * HEAD_DIM
    row = lax.broadcasted_iota(jnp.int32, (pair_w, blk), 0)
    causal = (lax.broadcasted_iota(jnp.int32, (blk, blk), 0)
              <= lax.broadcasted_iota(jnp.int32, (blk, blk), 1))

    heads = []
    for h in range(DIFF_HEADS):
        q_pair = qT_ref[h * pair_w:(h + 1) * pair_w, :]
        zero = jnp.zeros_like(q_pair)
        q_maps = (jnp.where(row < HEAD_DIM, q_pair, zero), jnp.where(row >= HEAD_DIM, q_pair, zero))

        def scores(kb, c):
            k_blk = k_ref[pl.ds(kb * blk, blk), h * pair_w:(h + 1) * pair_w]
            return _dot(k_blk, q_maps[c])

        def values(kb):
            return vT_ref[h * pair_w:(h + 1) * pair_w, pl.ds(kb * blk, blk)]

        v_blk = values(i)
        state = []
        for c in range(2):
            s = jnp.where(causal, scores(i, c), NEG_BIG)
            m = jnp.max(s, axis=0, keepdims=True)
            p = jnp.exp(s - m)
            l = jnp.sum(p, axis=0, keepdims=True)
            acc = _dot(v_blk, p.astype(BF16))
            state += [m, l, acc]

        def step(kb, st):
            v_b = values(kb)
            out = []
            for c in range(2):
                m_old, l_old, acc_old = st[3 * c:3 * c + 3]
                s = scores(kb, c)
                m = jnp.maximum(m_old, jnp.max(s, axis=0, keepdims=True))
                p = jnp.exp(s - m)
                scale = jnp.exp(m_old - m)
                l = scale * l_old + jnp.sum(p, axis=0, keepdims=True)
                acc = scale * acc_old + _dot(v_b, p.astype(BF16))
                out += [m, l, acc]
            return tuple(out)

        m0, l0, acc0, m1, l1, acc1 = lax.fori_loop(0, i, step, tuple(state))
        o = acc0 * (1.0 / l0) - lam * (acc1 * (1.0 / l1))
        ms = jnp.mean(o * o, axis=0, keepdims=True)
        heads.append(o * lax.rsqrt(ms + EPS) * subg_ref[...] * out_scale)

    y_b = jnp.concatenate(heads, axis=0).T.astype(BF16)
    merged = part_ref[...] + gb_ref[...].astype(F32) * _dot(
        y_b, wb1_ref[...])
    mix = _dot(merged.astype(BF16), wo_ref[...])
    o_ref[...] = _layer_norm(alpha * x_ref[...] + mix, g1_ref[...], b1_ref[...])


def _attn_call(lam, qT, k, vT, part, gb, x, subg, wb1, w_o, g1, b1, *, alpha, out_scale):
    bsz, s_len, d_model = x.shape
    blk = min(ATTN_BLOCK, s_len)
    grid = (bsz, s_len // blk)
    tok = lambda w: pl.BlockSpec((None, blk, w), lambda b, i: (b, i, 0))
    per_batch = lambda shape: pl.BlockSpec((None,) + shape, lambda b, i: (b, 0, 0),
                                           pipeline_mode=pl.Buffered(1))
    return pl.pallas_call(
        functools.partial(_attn_body, alpha=alpha, out_scale=out_scale),
        grid=grid,
        in_specs=[pl.BlockSpec(memory_space=pltpu.SMEM),
                  pl.BlockSpec((None, qT.shape[1], blk), lambda b, i: (b, 0, i)),
                  per_batch(k.shape[1:]), per_batch(vT.shape[1:]),
                  tok(d_model), tok(d_model), tok(d_model),
                  _resident(subg.shape), _resident(wb1.shape), _resident(w_o.shape),
                  _resident(g1.shape), _resident(b1.shape)],
        out_specs=tok(d_model),
        out_shape=jax.ShapeDtypeStruct((bsz, s_len, d_model), F32),
        compiler_params=pltpu.CompilerParams(
            dimension_semantics=("arbitrary", "arbitrary"), vmem_limit_bytes=VMEM_LIMIT),
        name="diff_attn",
    )(lam, qT, k, vT, part, gb, x, subg, wb1, w_o, g1, b1)


def _ffn_chunks(d_ff):
    step = 1024
    return [(lo, min(lo + step, d_ff)) for lo in range(0, d_ff, step)]


def _ffn_body(x_ref, wgu_ref, wd_ref, g_ref, b_ref, o_ref, *, alpha, d_ff):
    x = x_ref[...]
    xb = x.astype(BF16)
    acc = alpha * x
    for lo, hi in _ffn_chunks(d_ff):
        gate = _dot(xb, wgu_ref[:, lo:hi])
        up = _dot(xb, wgu_ref[:, d_ff + lo:d_ff + hi])
        hidden = (gate * jax.nn.sigmoid(gate) * up).astype(BF16)
        acc = acc + _dot(hidden, wd_ref[lo:hi, :])
    o_ref[...] = _layer_norm(acc, g_ref[...], b_ref[...])


def _ffn_call(x, w_gate_up, w_down, g, b, *, alpha):
    n_tok, d_model = x.shape
    d_ff = w_down.shape[0]
    tm = min(FFN_TILE, n_tok)
    tok = pl.BlockSpec((tm, d_model), lambda t: (t, 0))
    return pl.pallas_call(
        functools.partial(_ffn_body, alpha=alpha, d_ff=d_ff),
        grid=(n_tok // tm,),
        in_specs=[tok, _resident(w_gate_up.shape), _resident(w_down.shape),
                  _resident(g.shape), _resident(b.shape)],
        out_specs=tok,
        out_shape=jax.ShapeDtypeStruct((n_tok, d_model), F32),
        compiler_params=pltpu.CompilerParams(
            dimension_semantics=("arbitrary",), vmem_limit_bytes=VMEM_LIMIT),
        name="swiglu",
    )(x, w_gate_up, w_down, g, b)


def _rope_tables(positions):
    inv_freq = ROPE_THETA ** (-jnp.arange(0, ROPE_DIM, 2, dtype=F32) / ROPE_DIM)
    ang = positions.astype(F32)[..., None] * inv_freq
    cos, sin = jnp.cos(ang), jnp.sin(ang)
    rest = positions.shape + (HEAD_DIM - ROPE_DIM,)
    c_head = jnp.concatenate([cos, cos, jnp.ones(rest, F32)], axis=-1)
    s_head = jnp.concatenate([-sin, sin, jnp.zeros(rest, F32)], axis=-1)
    reps = LANES // HEAD_DIM
    return jnp.concatenate([c_head] * reps, axis=-1), jnp.concatenate([s_head] * reps, axis=-1)


def kernel(x, positions, w_in, w_sgu, b_sgu, sgu_ln_g, sgu_ln_b, lambda_q1, lambda_k1, lambda_q2,
           lambda_k2, subln_g, conv_w, w_branch, w_o, ln1_g, ln1_b, w_gate_up, w_down, ln2_g, ln2_b):
    bsz, s_len, d_model = x.shape
    depth = w_in.shape[0]
    half = d_model // 2
    alpha = (2 * depth) ** 0.25
    rope_c, rope_s = _rope_tables(positions)
    tril = jnp.tril(jnp.ones((CHUNK, CHUNK), dtype=bool))
    row2 = lambda a: a.reshape(1, -1)

    for l in range(depth):
        lambda_init = 0.8 - 0.6 * math.exp(-0.3 * l)
        lam = (jnp.exp(jnp.sum(lambda_q1[l] * lambda_k1[l])) - jnp.exp(jnp.sum(lambda_q2[l] * lambda_k2[l]))
               + lambda_init).reshape(1, 1).astype(F32)
        w_s = jnp.where(tril[None], w_sgu[l], 0.0).astype(BF16)
        b_s = jnp.repeat(b_sgu[l].T, half // SGU_GROUPS, axis=1)
        qT, k, vT, part, gb = _in_proj_call(
            x, rope_c, rope_s, w_in[l].astype(BF16), w_s, b_s, row2(sgu_ln_g[l]), row2(sgu_ln_b[l]),
            conv_w[l].T, w_branch[l, 0].astype(BF16), w_branch[l, 2].astype(BF16))
        x = _attn_call(lam, qT, k, vT, part, gb, x, subln_g[l].reshape(-1, 1),
                       w_branch[l, 1].astype(BF16), w_o[l].astype(BF16), row2(ln1_g[l]), row2(ln1_b[l]),
                       alpha=alpha, out_scale=1.0 - lambda_init)
        x = _ffn_call(x.reshape(bsz * s_len, d_model), w_gate_up[l].astype(BF16), w_down[l].astype(BF16),
                      row2(ln2_g[l]), row2(ln2_b[l]), alpha=alpha).reshape(bsz, s_len, d_model)
    return x
```

```python
import functools
import math

import jax
import jax.numpy as jnp
from jax import lax
from jax.experimental import pallas as pl
from jax.experimental.pallas import tpu as pltpu

F32 = jnp.float32
BF16 = jnp.bfloat16

CHUNK = 128
SGU_GROUPS = 4
DIFF_HEADS = 4
HEAD_DIM = 64
ROPE_DIM = HEAD_DIM // 4
ROPE_HALF = ROPE_DIM // 2
ROPE_THETA = 500000.0
CONV_K = 3
EPS = 1e-5
LANES = 128
NEG_BIG = -1e30

PROJ_TILE = 512
ATTN_Q_BLOCK = 512
ATTN_K_BLOCK = 256
FFN_TILE = 512
VMEM_LIMIT = 56 * 1024 * 1024


def _dot(a, b):
    return jnp.dot(a, b, preferred_element_type=F32)


def _layer_norm(x, g, b):
    mu = jnp.mean(x, axis=-1, keepdims=True)
    xc = x - mu
    var = jnp.mean(xc * xc, axis=-1, keepdims=True)
    return xc * lax.rsqrt(var + EPS) * g + b


def _resident(shape):
    nd = len(shape)
    return pl.BlockSpec(shape, lambda *_: (0,) * nd, pipeline_mode=pl.Buffered(1))


def _in_proj_body(x_ref, rc_ref, rs_ref, w_in_ref, wsgu_ref, bsgu_ref, lng_ref, lnb_ref,
                  convw_ref, wb0_ref, wb2_ref,
                  qT_ref, k_ref, vT_ref, part_ref, gb_ref, carry_ref, *, d_model):
    tm = x_ref.shape[0]
    half = d_model // 2
    t = pl.program_id(1)
    xb = x_ref[...].astype(BF16)

    def proj(lo, hi):
        return _dot(xb, w_in_ref[:, lo:hi])

    za = proj(0, d_model)
    za = 0.5 * za * (1.0 + lax.erf(za * (1.0 / math.sqrt(2.0))))
    u = za[:, :half]
    v = _layer_norm(za[:, half:], lng_ref[...], lnb_ref[...]).astype(BF16)
    gw = half // SGU_GROUPS
    rows = []
    for c in range(tm // CHUNK):
        cols = [_dot(wsgu_ref[g], v[c * CHUNK:(c + 1) * CHUNK, g * gw:(g + 1) * gw])
                for g in range(SGU_GROUPS)]
        rows.append(jnp.concatenate(cols, axis=1) + bsgu_ref[...])
    y_a = (u * jnp.concatenate(rows, axis=0)).astype(BF16)

    o1 = d_model
    qk_w = 2 * DIFF_HEADS * HEAD_DIM
    v_w = DIFF_HEADS * 2 * HEAD_DIM
    rc = jnp.concatenate([rc_ref[...]] * (qk_w // LANES), axis=1)
    rs = jnp.concatenate([rs_ref[...]] * (qk_w // LANES), axis=1)
    lane = lax.broadcasted_iota(jnp.int32, (tm, qk_w), 1)
    first = (lane % HEAD_DIM) < ROPE_HALF

    def rope(z):
        partner = jnp.where(first, pltpu.roll(z, qk_w - ROPE_HALF, 1), pltpu.roll(z, ROPE_HALF, 1))
        return z * rc + partner * rs

    q = rope(proj(o1, o1 + qk_w)) * (HEAD_DIM ** -0.5)
    qT_ref[...] = q.T.astype(BF16)
    k_ref[...] = rope(proj(o1 + qk_w, o1 + 2 * qk_w)).astype(BF16)
    o3 = o1 + 2 * qk_w
    vT_ref[...] = proj(o3, o3 + v_w).T.astype(BF16)

    o4 = o3 + v_w
    zc = proj(o4, o4 + 3 * half)
    h = zc[:, half:2 * half] * zc[:, 2 * half:]

    @pl.when(t == 0)
    def _():
        carry_ref[...] = jnp.zeros_like(carry_ref)

    prev = carry_ref[...]
    row = lax.broadcasted_iota(jnp.int32, (tm, half), 0)
    h1 = jnp.where(row == 0, prev[7:8, :], pltpu.roll(h, 1, 0))
    h2 = jnp.where(row == 0, prev[6:7, :], jnp.where(row == 1, prev[7:8, :], pltpu.roll(h, 2, 0)))
    carry_ref[...] = h[tm - 8:, :]
    conv = h2 * convw_ref[0:1, :] + h1 * convw_ref[1:2, :] + h * convw_ref[2:3, :]
    y_c = (zc[:, :half] * conv).astype(BF16)

    o5 = o4 + 3 * half
    g_a = jax.nn.sigmoid(proj(o5, o5 + d_model))
    g_c = jax.nn.sigmoid(proj(o5 + 2 * d_model, o5 + 3 * d_model))
    part = g_a * _dot(y_a, wb0_ref[...])
    part = part + g_c * _dot(y_c, wb2_ref[...])
    part_ref[...] = part
    gb_ref[...] = jax.nn.sigmoid(proj(o5 + d_model, o5 + 2 * d_model)).astype(BF16)


def _in_proj_call(x, rope_c, rope_s, w_in, w_sgu, b_sgu, ln_g, ln_b, conv_w, wb0, wb2):
    bsz, s_len, d_model = x.shape
    tm = min(PROJ_TILE, s_len)
    half = d_model // 2
    qk_w = 2 * DIFF_HEADS * HEAD_DIM
    v_w = DIFF_HEADS * 2 * HEAD_DIM
    grid = (bsz, s_len // tm)
    tok = lambda w: pl.BlockSpec((None, tm, w), lambda b, t: (b, t, 0))
    tokT = lambda w: pl.BlockSpec((None, w, tm), lambda b, t: (b, 0, t))
    return pl.pallas_call(
        functools.partial(_in_proj_body, d_model=d_model),
        grid=grid,
        in_specs=[tok(d_model), tok(LANES), tok(LANES),
                  _resident(w_in.shape), _resident(w_sgu.shape), _resident(b_sgu.shape),
                  _resident(ln_g.shape), _resident(ln_b.shape), _resident(conv_w.shape),
                  _resident(wb0.shape), _resident(wb2.shape)],
        out_specs=[tokT(qk_w), tok(qk_w), tokT(v_w), tok(d_model), tok(d_model)],
        out_shape=[jax.ShapeDtypeStruct((bsz, qk_w, s_len), BF16),
                   jax.ShapeDtypeStruct((bsz, s_len, qk_w), BF16),
                   jax.ShapeDtypeStruct((bsz, v_w, s_len), BF16),
                   jax.ShapeDtypeStruct((bsz, s_len, d_model), F32),
                   jax.ShapeDtypeStruct((bsz, s_len, d_model), BF16)],
        scratch_shapes=[pltpu.VMEM((8, half), F32)],
        compiler_params=pltpu.CompilerParams(
            dimension_semantics=("arbitrary", "arbitrary"), vmem_limit_bytes=VMEM_LIMIT),
        name="in_proj",
    )(x, rope_c, rope_s, w_in, w_sgu, b_sgu, ln_g, ln_b, conv_w, wb0, wb2)


def _attn_body(lam_ref, qT_ref, k_ref, vT_ref, part_ref, gb_ref, x_ref, subg_ref, wb1_ref,
               wo_ref, g1_ref, b1_ref, o_ref, sa_ref, sb_ref, acc_ref, *, alpha, out_scale):
    bk, bq = sa_ref.shape[1:]
    per_q = bq // bk
    i = pl.program_id(1)
    n_off = i * per_q
    lam = lam_ref[0, 0]
    pair_w = 2 * HEAD_DIM
    row = lax.broadcasted_iota(jnp.int32, (pair_w, bq), 0)
    key_i = lax.broadcasted_iota(jnp.int32, (bk, bq), 0)
    qry_i = lax.broadcasted_iota(jnp.int32, (bk, bq), 1)

    def colmax(s):
        return jnp.max(s, axis=0, keepdims=True)

    heads = []
    for h in range(DIFF_HEADS):
        q_pair = qT_ref[h * pair_w:(h + 1) * pair_w, :]
        zero = jnp.zeros_like(q_pair)
        q_maps = (jnp.where(row < HEAD_DIM, q_pair, zero), jnp.where(row >= HEAD_DIM, q_pair, zero))

        def scores(kb, c):
            k_blk = k_ref[pl.ds(pl.multiple_of(kb * bk, bk), bk), h * pair_w:(h + 1) * pair_w]
            return _dot(k_blk, q_maps[c])

        def values(kb):
            return vT_ref[h * pair_w:(h + 1) * pair_w, pl.ds(pl.multiple_of(kb * bk, bk), bk)]

        def stage(s_ref, kb):
            out = []
            for c in range(2):
                s = scores(kb, c)
                s_ref[c] = s
                out.append(colmax(s))
            return out

        def update(s_ref, c, kb, mb, m_run, l_run):
            m_new = jnp.maximum(m_run, mb)
            scale = jnp.exp(m_run - m_new)
            p = jnp.exp(s_ref[c] - m_new)
            l_new = scale * l_run + jnp.sum(p, axis=0, keepdims=True)
            acc_ref[c] = scale * acc_ref[c] + _dot(values(kb), p.astype(BF16))
            return m_new, l_new

        stats = [None, None]
        for d in range(per_q):
            kb = n_off + d
            visible = key_i + d * bk <= qry_i
            for c in range(2):
                s = jnp.where(visible, scores(kb, c), NEG_BIG)
                if d == 0:
                    m = colmax(s)
                    p = jnp.exp(s - m)
                    acc_ref[c] = _dot(values(kb), p.astype(BF16))
                    stats[c] = (m, jnp.sum(p, axis=0, keepdims=True))
                else:
                    sa_ref[c] = s
                    stats[c] = update(sa_ref, c, kb, colmax(s), *stats[c])

        mb_a = stage(sa_ref, 0)

        def pair(t, carry):
            m0, l0, m1, l1, mb_a0, mb_a1 = carry
            j = 2 * t
            mb_b = stage(sb_ref, j + 1)
            m0, l0 = update(sa_ref, 0, j, mb_a0, m0, l0)
            m1, l1 = update(sa_ref, 1, j, mb_a1, m1, l1)
            mb_n = stage(sa_ref, jnp.minimum(j + 2, jnp.maximum(n_off - 1, 0)))
            m0, l0 = update(sb_ref, 0, j + 1, mb_b[0], m0, l0)
            m1, l1 = update(sb_ref, 1, j + 1, mb_b[1], m1, l1)
            return m0, l0, m1, l1, mb_n[0], mb_n[1]

        m0, l0, m1, l1, _, _ = lax.fori_loop(
            0, n_off // 2, pair, (*stats[0], *stats[1], mb_a[0], mb_a[1]))
        o = acc_ref[0] * (1.0 / l0) - lam * (acc_ref[1] * (1.0 / l1))
        ms = jnp.mean(o * o, axis=0, keepdims=True)
        heads.append(o * lax.rsqrt(ms + EPS) * subg_ref[...] * out_scale)

    y_b = jnp.concatenate(heads, axis=0).T.astype(BF16)
    merged = part_ref[...] + gb_ref[...].astype(F32) * _dot(y_b, wb1_ref[...])
    mix = _dot(merged.astype(BF16), wo_ref[...])
    o_ref[...] = _layer_norm(alpha * x_ref[...] + mix, g1_ref[...], b1_ref[...])


def _attn_call(lam, qT, k, vT, part, gb, x, subg, wb1, w_o, g1, b1, *, alpha, out_scale):
    bsz, s_len, d_model = x.shape
    bq, bk = ATTN_Q_BLOCK, ATTN_K_BLOCK
    assert bq == 2 * bk and s_len % bq == 0
    grid = (bsz, s_len // bq)
    tok = lambda w: pl.BlockSpec((None, bq, w), lambda b, i: (b, i, 0))
    per_batch = lambda shape: pl.BlockSpec((None,) + shape, lambda b, i: (b, 0, 0),
                                           pipeline_mode=pl.Buffered(1))
    return pl.pallas_call(
        functools.partial(_attn_body, alpha=alpha, out_scale=out_scale),
        grid=grid,
        in_specs=[pl.BlockSpec(memory_space=pltpu.SMEM),
                  pl.BlockSpec((None, qT.shape[1], bq), lambda b, i: (b, 0, i)),
                  per_batch(k.shape[1:]), per_batch(vT.shape[1:]),
                  tok(d_model), tok(d_model), tok(d_model),
                  _resident(subg.shape), _resident(wb1.shape), _resident(w_o.shape),
                  _resident(g1.shape), _resident(b1.shape)],
        out_specs=tok(d_model),
        out_shape=jax.ShapeDtypeStruct((bsz, s_len, d_model), F32),
        scratch_shapes=[pltpu.VMEM((2, bk, bq), F32), pltpu.VMEM((2, bk, bq), F32),
                        pltpu.VMEM((2, 2 * HEAD_DIM, bq), F32)],
        compiler_params=pltpu.CompilerParams(
            dimension_semantics=("arbitrary", "arbitrary"), vmem_limit_bytes=VMEM_LIMIT),
        name="diff_attn",
    )(lam, qT, k, vT, part, gb, x, subg, wb1, w_o, g1, b1)


def _ffn_chunks(d_ff):
    step = 1024
    return [(lo, min(lo + step, d_ff)) for lo in range(0, d_ff, step)]


def _ffn_body(x_ref, wgu_ref, wd_ref, g_ref, b_ref, o_ref, *, alpha, d_ff):
    x = x_ref[...]
    xb = x.astype(BF16)
    acc = alpha * x
    for lo, hi in _ffn_chunks(d_ff):
        gate = _dot(xb, wgu_ref[:, lo:hi])
        up = _dot(xb, wgu_ref[:, d_ff + lo:d_ff + hi])
        hidden = (gate * jax.nn.sigmoid(gate) * up).astype(BF16)
        acc = acc + _dot(hidden, wd_ref[lo:hi, :])
    o_ref[...] = _layer_norm(acc, g_ref[...], b_ref[...])


def _ffn_call(x, w_gate_up, w_down, g, b, *, alpha):
    n_tok, d_model = x.shape
    d_ff = w_down.shape[0]
    tm = min(FFN_TILE, n_tok)
    tok = pl.BlockSpec((tm, d_model), lambda t: (t, 0))
    return pl.pallas_call(
        functools.partial(_ffn_body, alpha=alpha, d_ff=d_ff),
        grid=(n_tok // tm,),
        in_specs=[tok, _resident(w_gate_up.shape), _resident(w_down.shape),
                  _resident(g.shape), _resident(b.shape)],
        out_specs=tok,
        out_shape=jax.ShapeDtypeStruct((n_tok, d_model), F32),
        compiler_params=pltpu.CompilerParams(
            dimension_semantics=("arbitrary",), vmem_limit_bytes=VMEM_LIMIT),
        name="swiglu",
    )(x, w_gate_up, w_down, g, b)


def _rope_tables(positions):
    inv_freq = ROPE_THETA ** (-jnp.arange(0, ROPE_DIM, 2, dtype=F32) / ROPE_DIM)
    ang = positions.astype(F32)[..., None] * inv_freq
    cos, sin = jnp.cos(ang), jnp.sin(ang)
    rest = positions.shape + (HEAD_DIM - ROPE_DIM,)
    c_head = jnp.concatenate([cos, cos, jnp.ones(rest, F32)], axis=-1)
    s_head = jnp.concatenate([-sin, sin, jnp.zeros(rest, F32)], axis=-1)
    reps = LANES // HEAD_DIM
    return jnp.concatenate([c_head] * reps, axis=-1), jnp.concatenate([s_head] * reps, axis=-1)


def kernel(x, positions, w_in, w_sgu, b_sgu, sgu_ln_g, sgu_ln_b, lambda_q1, lambda_k1, lambda_q2,
           lambda_k2, subln_g, conv_w, w_branch, w_o, ln1_g, ln1_b, w_gate_up, w_down, ln2_g, ln2_b):
    bsz, s_len, d_model = x.shape
    depth = w_in.shape[0]
    half = d_model // 2
    alpha = (2 * depth) ** 0.25
    rope_c, rope_s = _rope_tables(positions)
    tril = jnp.tril(jnp.ones((CHUNK, CHUNK), dtype=bool))
    row2 = lambda a: a.reshape(1, -1)

    for l in range(depth):
        lambda_init = 0.8 - 0.6 * math.exp(-0.3 * l)
        lam = (jnp.exp(jnp.sum(lambda_q1[l] * lambda_k1[l])) - jnp.exp(jnp.sum(lambda_q2[l] * lambda_k2[l]))
               + lambda_init).reshape(1, 1).astype(F32)
        w_s = jnp.where(tril[None], w_sgu[l], 0.0).astype(BF16)
        b_s = jnp.repeat(b_sgu[l].T, half // SGU_GROUPS, axis=1)
        qT, k, vT, part, gb = _in_proj_call(
            x, rope_c, rope_s, w_in[l].astype(BF16), w_s, b_s, row2(sgu_ln_g[l]), row2(sgu_ln_b[l]),
            conv_w[l].T, w_branch[l, 0].astype(BF16), w_branch[l, 2].astype(BF16))
        x = _attn_call(lam, qT, k, vT, part, gb, x, subln_g[l].reshape(-1, 1),
                       w_branch[l, 1].astype(BF16), w_o[l].astype(BF16), row2(ln1_g[l]), row2(ln1_b[l]),
                       alpha=alpha, out_scale=1.0 - lambda_init)
        x = _ffn_call(x.reshape(bsz * s_len, d_model), w_gate_up[l].astype(BF16), w_down[l].astype(BF16),
                      row2(ln2_g[l]), row2(ln2_b[l]), alpha=alpha).reshape(bsz, s_len, d_model)
    return x
```

```python
import functools
import math

import jax
import jax.numpy as jnp
from jax import lax
from jax.experimental import pallas as pl
from jax.experimental.pallas import tpu as pltpu

F32 = jnp.float32
BF16 = jnp.bfloat16

CHUNK = 128
SGU_GROUPS = 4
DIFF_HEADS = 4
HEAD_DIM = 64
ROPE_DIM = HEAD_DIM // 4
ROPE_HALF = ROPE_DIM // 2
ROPE_THETA = 500000.0
CONV_K = 3
EPS = 1e-5
LANES = 128
LOG2E = 1.4426950408889634
NEG_BIG = -1e30
SUM_ROWS = 16

PROJ_TILE = 512
ATTN_Q_BLOCK = 512
ATTN_K_BLOCK = 256
FFN_TILE = 512
VMEM_LIMIT = 56 * 1024 * 1024


def _dot(a, b):
    return jnp.dot(a, b, preferred_element_type=F32)


def _layer_norm(x, g, b):
    mu = jnp.mean(x, axis=-1, keepdims=True)
    xc = x - mu
    var = jnp.mean(xc * xc, axis=-1, keepdims=True)
    return xc * lax.rsqrt(var + EPS) * g + b


def _resident(shape):
    nd = len(shape)
    return pl.BlockSpec(shape, lambda *_: (0,) * nd, pipeline_mode=pl.Buffered(1))


def _in_proj_body(x_ref, rc_ref, rs_ref, w_in_ref, wsgu_ref, bsgu_ref, lng_ref, lnb_ref,
                  convw_ref, wb0_ref, wb2_ref,
                  qT_ref, k_ref, vT_ref, part_ref, gb_ref, carry_ref, *, d_model):
    tm = x_ref.shape[0]
    half = d_model // 2
    t = pl.program_id(1)
    xb = x_ref[...].astype(BF16)

    def proj(lo, hi):
        return _dot(xb, w_in_ref[:, lo:hi])

    za = proj(0, d_model)
    za = 0.5 * za * (1.0 + lax.erf(za * (1.0 / math.sqrt(2.0))))
    u = za[:, :half]
    v = _layer_norm(za[:, half:], lng_ref[...], lnb_ref[...]).astype(BF16)
    gw = half // SGU_GROUPS
    rows = []
    for c in range(tm // CHUNK):
        cols = [_dot(wsgu_ref[g], v[c * CHUNK:(c + 1) * CHUNK, g * gw:(g + 1) * gw])
                for g in range(SGU_GROUPS)]
        rows.append(jnp.concatenate(cols, axis=1) + bsgu_ref[...])
    y_a = (u * jnp.concatenate(rows, axis=0)).astype(BF16)

    o1 = d_model
    qk_w = 2 * DIFF_HEADS * HEAD_DIM
    v_w = DIFF_HEADS * 2 * HEAD_DIM
    rc = jnp.concatenate([rc_ref[...]] * (qk_w // LANES), axis=1)
    rs = jnp.concatenate([rs_ref[...]] * (qk_w // LANES), axis=1)
    lane = lax.broadcasted_iota(jnp.int32, (tm, qk_w), 1)
    first = (lane % HEAD_DIM) < ROPE_HALF

    def rope(z):
        partner = jnp.where(first, pltpu.roll(z, qk_w - ROPE_HALF, 1), pltpu.roll(z, ROPE_HALF, 1))
        return z * rc + partner * rs

    q = rope(proj(o1, o1 + qk_w)) * (HEAD_DIM ** -0.5 * LOG2E)
    qT_ref[...] = q.T.astype(BF16)
    k_ref[...] = rope(proj(o1 + qk_w, o1 + 2 * qk_w)).astype(BF16)
    o3 = o1 + 2 * qk_w
    vT_ref[...] = proj(o3, o3 + v_w).T.astype(BF16)

    o4 = o3 + v_w
    zc = proj(o4, o4 + 3 * half)
    h = zc[:, half:2 * half] * zc[:, 2 * half:]

    @pl.when(t == 0)
    def _():
        carry_ref[...] = jnp.zeros_like(carry_ref)

    prev = carry_ref[...]
    row = lax.broadcasted_iota(jnp.int32, (tm, half), 0)
    h1 = jnp.where(row == 0, prev[7:8, :], pltpu.roll(h, 1, 0))
    h2 = jnp.where(row == 0, prev[6:7, :], jnp.where(row == 1, prev[7:8, :], pltpu.roll(h, 2, 0)))
    carry_ref[...] = h[tm - 8:, :]
    conv = h2 * convw_ref[0:1, :] + h1 * convw_ref[1:2, :] + h * convw_ref[2:3, :]
    y_c = (zc[:, :half] * conv).astype(BF16)

    o5 = o4 + 3 * half
    g_a = jax.nn.sigmoid(proj(o5, o5 + d_model))
    g_c = jax.nn.sigmoid(proj(o5 + 2 * d_model, o5 + 3 * d_model))
    part = g_a * _dot(y_a, wb0_ref[...])
    part = part + g_c * _dot(y_c, wb2_ref[...])
    part_ref[...] = part
    gb_ref[...] = jax.nn.sigmoid(proj(o5 + d_model, o5 + 2 * d_model)).astype(BF16)


def _in_proj_call(x, rope_c, rope_s, w_in, w_sgu, b_sgu, ln_g, ln_b, conv_w, wb0, wb2):
    bsz, s_len, d_model = x.shape
    tm = min(PROJ_TILE, s_len)
    half = d_model // 2
    qk_w = 2 * DIFF_HEADS * HEAD_DIM
    v_w = DIFF_HEADS * 2 * HEAD_DIM
    grid = (bsz, s_len // tm)
    tok = lambda w: pl.BlockSpec((None, tm, w), lambda b, t: (b, t, 0))
    tokT = lambda w: pl.BlockSpec((None, w, tm), lambda b, t: (b, 0, t))
    return pl.pallas_call(
        functools.partial(_in_proj_body, d_model=d_model),
        grid=grid,
        in_specs=[tok(d_model), tok(LANES), tok(LANES),
                  _resident(w_in.shape), _resident(w_sgu.shape), _resident(b_sgu.shape),
                  _resident(ln_g.shape), _resident(ln_b.shape), _resident(conv_w.shape),
                  _resident(wb0.shape), _resident(wb2.shape)],
        out_specs=[tokT(qk_w), tok(qk_w), tokT(v_w), tok(d_model), tok(d_model)],
        out_shape=[jax.ShapeDtypeStruct((bsz, qk_w, s_len), BF16),
                   jax.ShapeDtypeStruct((bsz, s_len, qk_w), BF16),
                   jax.ShapeDtypeStruct((bsz, v_w, s_len), BF16),
                   jax.ShapeDtypeStruct((bsz, s_len, d_model), F32),
                   jax.ShapeDtypeStruct((bsz, s_len, d_model), BF16)],
        scratch_shapes=[pltpu.VMEM((8, half), F32)],
        compiler_params=pltpu.CompilerParams(
            dimension_semantics=("arbitrary", "arbitrary"), vmem_limit_bytes=VMEM_LIMIT),
        name="in_proj",
    )(x, rope_c, rope_s, w_in, w_sgu, b_sgu, ln_g, ln_b, conv_w, wb0, wb2)


def _attn_body(lam_ref, qT_ref, k_ref, vT_ref, part_ref, gb_ref, x_ref, subg_ref, wb1_ref,
               wo_ref, g1_ref, b1_ref, o_ref, qm_ref, sa_ref, sb_ref, acc_ref, m_ref, mba_ref, mbb_ref,
               *, alpha, out_scale):
    n_maps, bk, bq = sa_ref.shape
    assert bq == 2 * bk and n_maps == 2 * DIFF_HEADS
    half = bq // 2
    i = pl.program_id(1)
    n_off = 2 * i
    lam = lam_ref[0, 0]
    pair_w = 2 * HEAD_DIM
    row = lax.broadcasted_iota(jnp.int32, (pair_w, bq), 0)
    visible = (lax.broadcasted_iota(jnp.int32, (bk, bq), 0)
               <= lax.broadcasted_iota(jnp.int32, (bk, bq), 1))

    def colmax(s):
        return jnp.max(s, axis=0, keepdims=True)

    def head_rows(h):
        return slice(h * pair_w, (h + 1) * pair_w)

    def keys(kb):
        return pl.ds(pl.multiple_of(kb * bk, bk), bk)

    def values(h, kb):
        return jnp.concatenate([vT_ref[head_rows(h), keys(kb)], jnp.ones((SUM_ROWS, bk), BF16)], axis=0)

    def stage(s_ref, mb_ref, idx, kb):
        s = _dot(k_ref[keys(kb), head_rows(idx // 2)], qm_ref[idx])
        s_ref[idx] = s
        mb_ref[idx] = colmax(s)

    def update(s, idx, kb, mb, q_lo=0):
        m_run = m_ref[idx, :, q_lo:]
        m_new = jnp.maximum(m_run, mb)
        p = jnp.exp2(s - m_new).astype(BF16)
        acc_ref[idx, :, q_lo:] = (jnp.exp2(m_run - m_new) * acc_ref[idx, :, q_lo:]
                                  + _dot(values(idx // 2, kb), p))
        m_ref[idx, :, q_lo:] = m_new

    for h in range(DIFF_HEADS):
        q_pair = qT_ref[head_rows(h), :]
        zero = jnp.zeros_like(q_pair)
        qm_ref[2 * h] = jnp.where(row < HEAD_DIM, q_pair, zero)
        qm_ref[2 * h + 1] = jnp.where(row >= HEAD_DIM, q_pair, zero)
    acc_ref[...] = jnp.zeros_like(acc_ref)
    m_ref[...] = jnp.full_like(m_ref, NEG_BIG)
    for idx in range(n_maps):
        stage(sa_ref, mba_ref, idx, 0)

    @pl.loop(0, i)
    def _(t):
        j = 2 * t
        for idx in range(n_maps):
            stage(sb_ref, mbb_ref, idx, j + 1)
            update(sa_ref[idx], idx, j, mba_ref[idx])
        for idx in range(n_maps):
            stage(sa_ref, mba_ref, idx, j + 2)
            update(sb_ref[idx], idx, j + 1, mbb_ref[idx])

    for idx in range(n_maps):
        k_blk = k_ref[keys(n_off + 1), head_rows(idx // 2)]
        sb_ref[idx, :, half:] = jnp.where(visible[:, :half], _dot(k_blk, qm_ref[idx, :, half:]), NEG_BIG)
    for idx in range(n_maps):
        s = jnp.where(visible, sa_ref[idx], NEG_BIG)
        update(s, idx, n_off, colmax(s))
    for idx in range(n_maps):
        s = sb_ref[idx, :, half:]
        update(s, idx, n_off + 1, colmax(s), half)

    heads = []
    for h in range(DIFF_HEADS):
        o_maps = [acc_ref[idx, :pair_w, :] * (1.0 / acc_ref[idx, pair_w:pair_w + 1, :])
                  for idx in (2 * h, 2 * h + 1)]
        o = o_maps[0] - lam * o_maps[1]
        ms = jnp.mean(o * o, axis=0, keepdims=True)
        heads.append(o * lax.rsqrt(ms + EPS) * subg_ref[...] * out_scale)

    y_b = jnp.concatenate(heads, axis=0).T.astype(BF16)
    merged = part_ref[...] + gb_ref[...].astype(F32) * _dot(y_b, wb1_ref[...])
    mix = _dot(merged.astype(BF16), wo_ref[...])
    o_ref[...] = _layer_norm(alpha * x_ref[...] + mix, g1_ref[...], b1_ref[...])


def _attn_call(lam, qT, k, vT, part, gb, x, subg, wb1, w_o, g1, b1, *, alpha, out_scale):
    bsz, s_len, d_model = x.shape
    bq, bk = ATTN_Q_BLOCK, ATTN_K_BLOCK
    assert bq == 2 * bk and s_len % bq == 0
    n_maps = 2 * DIFF_HEADS
    grid = (bsz, s_len // bq)
    tok = lambda w: pl.BlockSpec((None, bq, w), lambda b, i: (b, i, 0))
    per_batch = lambda shape: pl.BlockSpec((None,) + shape, lambda b, i: (b, 0, 0),
                                           pipeline_mode=pl.Buffered(1))
    return pl.pallas_call(
        functools.partial(_attn_body, alpha=alpha, out_scale=out_scale),
        grid=grid,
        in_specs=[pl.BlockSpec(memory_space=pltpu.SMEM),
                  pl.BlockSpec((None, qT.shape[1], bq), lambda b, i: (b, 0, i)),
                  per_batch(k.shape[1:]), per_batch(vT.shape[1:]),
                  tok(d_model), tok(d_model), tok(d_model),
                  _resident(subg.shape), _resident(wb1.shape), _resident(w_o.shape),
                  _resident(g1.shape), _resident(b1.shape)],
        out_specs=tok(d_model),
        out_shape=jax.ShapeDtypeStruct((bsz, s_len, d_model), F32),
        scratch_shapes=[pltpu.VMEM((n_maps, 2 * HEAD_DIM, bq), BF16),
                        pltpu.VMEM((n_maps, bk, bq), F32), pltpu.VMEM((n_maps, bk, bq), F32),
                        pltpu.VMEM((n_maps, 2 * HEAD_DIM + SUM_ROWS, bq), F32),
                        pltpu.VMEM((n_maps, 1, bq), F32), pltpu.VMEM((n_maps, 1, bq), F32),
                        pltpu.VMEM((n_maps, 1, bq), F32)],
        compiler_params=pltpu.CompilerParams(
            dimension_semantics=("arbitrary", "arbitrary"), vmem_limit_bytes=VMEM_LIMIT),
        name="diff_attn",
    )(lam, qT, k, vT, part, gb, x, subg, wb1, w_o, g1, b1)


def _ffn_chunks(d_ff):
    step = 1024
    return [(lo, min(lo + step, d_ff)) for lo in range(0, d_ff, step)]


def _ffn_body(x_ref, wgu_ref, wd_ref, g_ref, b_ref, o_ref, *, alpha, d_ff):
    x = x_ref[...]
    xb = x.astype(BF16)
    acc = alpha * x
    for lo, hi in _ffn_chunks(d_ff):
        gate = _dot(xb, wgu_ref[:, lo:hi])
        up = _dot(xb, wgu_ref[:, d_ff + lo:d_ff + hi])
        hidden = (gate * jax.nn.sigmoid(gate) * up).astype(BF16)
        acc = acc + _dot(hidden, wd_ref[lo:hi, :])
    o_ref[...] = _layer_norm(acc, g_ref[...], b_ref[...])


def _ffn_call(x, w_gate_up, w_down, g, b, *, alpha):
    n_tok, d_model = x.shape
    d_ff = w_down.shape[0]
    tm = min(FFN_TILE, n_tok)
    tok = pl.BlockSpec((tm, d_model), lambda t: (t, 0))
    return pl.pallas_call(
        functools.partial(_ffn_body, alpha=alpha, d_ff=d_ff),
        grid=(n_tok // tm,),
        in_specs=[tok, _resident(w_gate_up.shape), _resident(w_down.shape),
                  _resident(g.shape), _resident(b.shape)],
        out_specs=tok,
        out_shape=jax.ShapeDtypeStruct((n_tok, d_model), F32),
        compiler_params=pltpu.CompilerParams(
            dimension_semantics=("arbitrary",), vmem_limit_bytes=VMEM_LIMIT),
        name="swiglu",
    )(x, w_gate_up, w_down, g, b)


def _rope_tables(positions):
    inv_freq = ROPE_THETA ** (-jnp.arange(0, ROPE_DIM, 2, dtype=F32) / ROPE_DIM)
    ang = positions.astype(F32)[..., None] * inv_freq
    cos, sin = jnp.cos(ang), jnp.sin(ang)
    rest = positions.shape + (HEAD_DIM - ROPE_DIM,)
    c_head = jnp.concatenate([cos, cos, jnp.ones(rest, F32)], axis=-1)
    s_head = jnp.concatenate([-sin, sin, jnp.zeros(rest, F32)], axis=-1)
    reps = LANES // HEAD_DIM
    return jnp.concatenate([c_head] * reps, axis=-1), jnp.concatenate([s_head] * reps, axis=-1)


def kernel(x, positions, w_in, w_sgu, b_sgu, sgu_ln_g, sgu_ln_b, lambda_q1, lambda_k1, lambda_q2,
           lambda_k2, subln_g, conv_w, w_branch, w_o, ln1_g, ln1_b, w_gate_up, w_down, ln2_g, ln2_b):
    bsz, s_len, d_model = x.shape
    depth = w_in.shape[0]
    half = d_model // 2
    alpha = (2 * depth) ** 0.25
    rope_c, rope_s = _rope_tables(positions)
    tril = jnp.tril(jnp.ones((CHUNK, CHUNK), dtype=bool))
    row2 = lambda a: a.reshape(1, -1)

    for l in range(depth):
        lambda_init = 0.8 - 0.6 * math.exp(-0.3 * l)
        lam = (jnp.exp(jnp.sum(lambda_q1[l] * lambda_k1[l])) - jnp.exp(jnp.sum(lambda_q2[l] * lambda_k2[l]))
               + lambda_init).reshape(1, 1).astype(F32)
        w_s = jnp.where(tril[None], w_sgu[l], 0.0).astype(BF16)
        b_s = jnp.repeat(b_sgu[l].T, half // SGU_GROUPS, axis=1)
        qT, k, vT, part, gb = _in_proj_call(
            x, rope_c, rope_s, w_in[l].astype(BF16), w_s, b_s, row2(sgu_ln_g[l]), row2(sgu_ln_b[l]),
            conv_w[l].T, w_branch[l, 0].astype(BF16), w_branch[l, 2].astype(BF16))
        x = _attn_call(lam, qT, k, vT, part, gb, x, subln_g[l].reshape(-1, 1),
                       w_branch[l, 1].astype(BF16), w_o[l].astype(BF16), row2(ln1_g[l]), row2(ln1_b[l]),
                       alpha=alpha, out_scale=1.0 - lambda_init)
        x = _ffn_call(x.reshape(bsz * s_len, d_model), w_gate_up[l].astype(BF16), w_down[l].astype(BF16),
                      row2(ln2_g[l]), row2(ln2_b[l]), alpha=alpha).reshape(bsz, s_len, d_model)
    return x
```

```python
import functools
import math

import jax
import jax.numpy as jnp
from jax import lax
from jax.experimental import pallas as pl
from jax.experimental.pallas import tpu as pltpu

F32 = jnp.float32
BF16 = jnp.bfloat16

CHUNK = 128
SGU_GROUPS = 4
DIFF_HEADS = 4
HEAD_DIM = 64
ROPE_DIM = HEAD_DIM // 4
ROPE_HALF = ROPE_DIM // 2
ROPE_THETA = 500000.0
CONV_K = 3
EPS = 1e-5
LOG2E = 1.4426950408889634
NEG_BIG = -1e30
SUM_ROWS = 16

PROJ_TILE = 512
ATTN_Q_BLOCK = 512
ATTN_K_BLOCK = 256
FFN_TILE = 512
VMEM_LIMIT = 56 * 1024 * 1024


def _dot(a, b):
    return jnp.dot(a, b, preferred_element_type=F32)


def _layer_norm(x, g, b):
    mu = jnp.mean(x, axis=-1, keepdims=True)
    xc = x - mu
    var = jnp.mean(xc * xc, axis=-1, keepdims=True)
    return xc * lax.rsqrt(var + EPS) * g + b


def _resident(shape):
    nd = len(shape)
    return pl.BlockSpec(shape, lambda *_: (0,) * nd, pipeline_mode=pl.Buffered(1))


def _in_proj_body(x_ref, cos_ref, sin_ref, w_in_ref, wsgu_ref, bsgu_ref, lng_ref, lnb_ref,
                  convw_ref, wb0_ref, wb2_ref,
                  qT_ref, k_ref, vT_ref, part_ref, gb_ref, carry_ref, *, d_model):
    tm = x_ref.shape[0]
    half = d_model // 2
    t = pl.program_id(1)
    xb = x_ref[...].astype(BF16)

    def proj(lo, hi):
        return _dot(xb, w_in_ref[:, lo:hi])

    o1 = d_model
    qk_w = 2 * DIFF_HEADS * HEAD_DIM
    v_w = DIFF_HEADS * 2 * HEAD_DIM
    o3 = o1 + 2 * qk_w
    o4 = o3 + v_w
    o5 = o4 + 3 * half

    za = proj(0, d_model)
    zq_t = proj(o1, o1 + qk_w).T
    zk_t = proj(o1 + qk_w, o3).T

    za = 0.5 * za * (1.0 + lax.erf(za * (1.0 / math.sqrt(2.0))))
    u = za[:, :half]
    v = _layer_norm(za[:, half:], lng_ref[...], lnb_ref[...]).astype(BF16)

    cos, sin = cos_ref[...], sin_ref[...]

    def rope_t(z_t):
        pieces = []
        for lo in range(0, qk_w, HEAD_DIM):
            r1, r2 = z_t[lo:lo + ROPE_HALF], z_t[lo + ROPE_HALF:lo + ROPE_DIM]
            pieces += [r1 * cos - r2 * sin, r2 * cos + r1 * sin, z_t[lo + ROPE_DIM:lo + HEAD_DIM]]
        return jnp.concatenate(pieces, axis=0)

    q_t = (rope_t(zq_t) * (HEAD_DIM ** -0.5 * LOG2E)).astype(BF16)
    no_q = jnp.zeros((HEAD_DIM, tm), BF16)
    maps = []
    for lo in range(0, qk_w, 2 * HEAD_DIM):
        maps += [q_t[lo:lo + HEAD_DIM], no_q, no_q, q_t[lo + HEAD_DIM:lo + 2 * HEAD_DIM]]
    qT_ref[...] = jnp.concatenate(maps, axis=0)
    k_ref[...] = rope_t(zk_t).T.astype(BF16)
    vT_ref[...] = proj(o3, o4).T.astype(BF16)
    gb_ref[...] = jax.nn.sigmoid(proj(o5 + d_model, o5 + 2 * d_model)).astype(BF16)

    gw = half // SGU_GROUPS
    rows = []
    for c in range(tm // CHUNK):
        cols = [_dot(wsgu_ref[g], v[c * CHUNK:(c + 1) * CHUNK, g * gw:(g + 1) * gw])
                for g in range(SGU_GROUPS)]
        rows.append(jnp.concatenate(cols, axis=1) + bsgu_ref[...])
    y_a = (u * jnp.concatenate(rows, axis=0)).astype(BF16)

    zc = proj(o4, o5)
    h = zc[:, half:2 * half] * zc[:, 2 * half:]

    @pl.when(t == 0)
    def _():
        carry_ref[...] = jnp.zeros_like(carry_ref)

    prev = carry_ref[...]
    row = lax.broadcasted_iota(jnp.int32, (tm, half), 0)
    h1 = jnp.where(row == 0, prev[7:8, :], pltpu.roll(h, 1, 0))
    h2 = jnp.where(row == 0, prev[6:7, :], jnp.where(row == 1, prev[7:8, :], pltpu.roll(h, 2, 0)))
    carry_ref[...] = h[tm - 8:, :]
    conv = h2 * convw_ref[0:1, :] + h1 * convw_ref[1:2, :] + h * convw_ref[2:3, :]
    y_c = (zc[:, :half] * conv).astype(BF16)

    g_a = jax.nn.sigmoid(proj(o5, o5 + d_model))
    g_c = jax.nn.sigmoid(proj(o5 + 2 * d_model, o5 + 3 * d_model))
    part_ref[...] = g_a * _dot(y_a, wb0_ref[...]) + g_c * _dot(y_c, wb2_ref[...])


def _in_proj_call(x, rope_cos, rope_sin, w_in, w_sgu, b_sgu, ln_g, ln_b, conv_w, wb0, wb2):
    bsz, s_len, d_model = x.shape
    tm = min(PROJ_TILE, s_len)
    half = d_model // 2
    qk_w = 2 * DIFF_HEADS * HEAD_DIM
    v_w = DIFF_HEADS * 2 * HEAD_DIM
    grid = (bsz, s_len // tm)
    tok = lambda w: pl.BlockSpec((None, tm, w), lambda b, t: (b, t, 0))
    tokT = lambda w: pl.BlockSpec((None, w, tm), lambda b, t: (b, 0, t))
    return pl.pallas_call(
        functools.partial(_in_proj_body, d_model=d_model),
        grid=grid,
        in_specs=[tok(d_model), tokT(ROPE_HALF), tokT(ROPE_HALF),
                  _resident(w_in.shape), _resident(w_sgu.shape), _resident(b_sgu.shape),
                  _resident(ln_g.shape), _resident(ln_b.shape), _resident(conv_w.shape),
                  _resident(wb0.shape), _resident(wb2.shape)],
        out_specs=[tokT(2 * qk_w), tok(qk_w), tokT(v_w), tok(d_model), tok(d_model)],
        out_shape=[jax.ShapeDtypeStruct((bsz, 2 * qk_w, s_len), BF16),
                   jax.ShapeDtypeStruct((bsz, s_len, qk_w), BF16),
                   jax.ShapeDtypeStruct((bsz, v_w, s_len), BF16),
                   jax.ShapeDtypeStruct((bsz, s_len, d_model), F32),
                   jax.ShapeDtypeStruct((bsz, s_len, d_model), BF16)],
        scratch_shapes=[pltpu.VMEM((8, half), F32)],
        compiler_params=pltpu.CompilerParams(
            dimension_semantics=("arbitrary", "arbitrary"), vmem_limit_bytes=VMEM_LIMIT),
        name="in_proj",
    )(x, rope_cos, rope_sin, w_in, w_sgu, b_sgu, ln_g, ln_b, conv_w, wb0, wb2)


def _attn_body(lam_ref, qm_ref, k_ref, vT_ref, part_ref, gb_ref, x_ref, subg_ref, wb1_ref,
               wo_ref, g1_ref, b1_ref, o_ref, sa_ref, sb_ref, acc_ref, m_ref, mba_ref, mbb_ref,
               *, alpha, out_scale):
    n_maps, bk, bq = sa_ref.shape
    assert bq == 2 * bk and n_maps == 2 * DIFF_HEADS
    half = bq // 2
    i = pl.program_id(1)
    n_off = 2 * i
    lam = lam_ref[0, 0]
    pair_w = 2 * HEAD_DIM
    visible = (lax.broadcasted_iota(jnp.int32, (bk, bq), 0)
               <= lax.broadcasted_iota(jnp.int32, (bk, bq), 1))

    def colmax(s):
        return jnp.max(s, axis=0, keepdims=True)

    def rows(r):
        return slice(r * pair_w, (r + 1) * pair_w)

    def keys(kb):
        return pl.ds(pl.multiple_of(kb * bk, bk), bk)

    def values(h, kb):
        return jnp.concatenate([vT_ref[rows(h), keys(kb)], jnp.ones((SUM_ROWS, bk), BF16)], axis=0)

    def scores(idx, kb, q_lo=0):
        return _dot(k_ref[keys(kb), rows(idx // 2)], qm_ref[rows(idx), q_lo:])

    def stage(s_ref, mb_ref, idx, kb):
        s = scores(idx, kb)
        s_ref[idx] = s
        mb_ref[idx] = colmax(s)

    def update(s, idx, kb, mb, q_lo=0):
        m_run = m_ref[idx, :, q_lo:]
        m_new = jnp.maximum(m_run, mb)
        p = jnp.exp2(s - m_new).astype(BF16)
        acc_ref[idx, :, q_lo:] = (jnp.exp2(m_run - m_new) * acc_ref[idx, :, q_lo:]
                                  + _dot(values(idx // 2, kb), p))
        m_ref[idx, :, q_lo:] = m_new

    for idx in range(n_maps):
        stage(sa_ref, mba_ref, idx, 0)
    acc_ref[...] = jnp.zeros_like(acc_ref)
    m_ref[...] = jnp.full_like(m_ref, NEG_BIG)

    @pl.loop(0, i)
    def _(t):
        j = 2 * t
        for idx in range(n_maps):
            stage(sb_ref, mbb_ref, idx, j + 1)
            update(sa_ref[idx], idx, j, mba_ref[idx])
        for idx in range(n_maps):
            stage(sa_ref, mba_ref, idx, j + 2)
            update(sb_ref[idx], idx, j + 1, mbb_ref[idx])

    for idx in range(n_maps):
        sb_ref[idx, :, half:] = jnp.where(visible[:, :half], scores(idx, n_off + 1, half), NEG_BIG)
        s = jnp.where(visible, sa_ref[idx], NEG_BIG)
        update(s, idx, n_off, colmax(s))
    heads = []
    for h in range(DIFF_HEADS):
        o_maps = []
        for idx in (2 * h, 2 * h + 1):
            s = sb_ref[idx, :, half:]
            update(s, idx, n_off + 1, colmax(s), half)
            o_maps.append(acc_ref[idx, :pair_w, :] * (1.0 / acc_ref[idx, pair_w:pair_w + 1, :]))
        o = o_maps[0] - lam * o_maps[1]
        ms = jnp.mean(o * o, axis=0, keepdims=True)
        heads.append(o * lax.rsqrt(ms + EPS) * subg_ref[...] * out_scale)

    y_bt = jnp.concatenate(heads, axis=0)
    for lo in (0, half):
        qs = slice(lo, lo + half)
        y_b = y_bt[:, qs].T.astype(BF16)
        merged = part_ref[qs, :] + gb_ref[qs, :].astype(F32) * _dot(y_b, wb1_ref[...])
        mix = _dot(merged.astype(BF16), wo_ref[...])
        o_ref[qs, :] = _layer_norm(alpha * x_ref[qs, :] + mix, g1_ref[...], b1_ref[...])


def _attn_call(lam, qT, k, vT, part, gb, x, subg, wb1, w_o, g1, b1, *, alpha, out_scale):
    bsz, s_len, d_model = x.shape
    bq, bk = ATTN_Q_BLOCK, ATTN_K_BLOCK
    assert bq == 2 * bk and s_len % bq == 0
    n_maps = 2 * DIFF_HEADS
    grid = (bsz, s_len // bq)
    tok = lambda w: pl.BlockSpec((None, bq, w), lambda b, i: (b, i, 0))
    per_batch = lambda shape: pl.BlockSpec((None,) + shape, lambda b, i: (b, 0, 0),
                                           pipeline_mode=pl.Buffered(1))
    return pl.pallas_call(
        functools.partial(_attn_body, alpha=alpha, out_scale=out_scale),
        grid=grid,
        in_specs=[pl.BlockSpec(memory_space=pltpu.SMEM),
                  pl.BlockSpec((None, qT.shape[1], bq), lambda b, i: (b, 0, i)),
                  per_batch(k.shape[1:]), per_batch(vT.shape[1:]),
                  tok(d_model), tok(d_model), tok(d_model),
                  _resident(subg.shape), _resident(wb1.shape), _resident(w_o.shape),
                  _resident(g1.shape), _resident(b1.shape)],
        out_specs=tok(d_model),
        out_shape=jax.ShapeDtypeStruct((bsz, s_len, d_model), F32),
        scratch_shapes=[pltpu.VMEM((n_maps, bk, bq), F32), pltpu.VMEM((n_maps, bk, bq), F32),
                        pltpu.VMEM((n_maps, 2 * HEAD_DIM + SUM_ROWS, bq), F32),
                        pltpu.VMEM((n_maps, 1, bq), F32), pltpu.VMEM((n_maps, 1, bq), F32),
                        pltpu.VMEM((n_maps, 1, bq), F32)],
        compiler_params=pltpu.CompilerParams(
            dimension_semantics=("arbitrary", "arbitrary"), vmem_limit_bytes=VMEM_LIMIT),
        name="diff_attn",
    )(lam, qT, k, vT, part, gb, x, subg, wb1, w_o, g1, b1)


def _ffn_chunks(d_ff):
    step = 1024
    return [(lo, min(lo + step, d_ff)) for lo in range(0, d_ff, step)]


def _ffn_body(x_ref, wgu_ref, wd_ref, g_ref, b_ref, o_ref, *, alpha, d_ff):
    x = x_ref[...]
    xb = x.astype(BF16)
    acc = alpha * x
    for lo, hi in _ffn_chunks(d_ff):
        gate = _dot(xb, wgu_ref[:, lo:hi])
        up = _dot(xb, wgu_ref[:, d_ff + lo:d_ff + hi])
        hidden = (gate * jax.nn.sigmoid(gate) * up).astype(BF16)
        acc = acc + _dot(hidden, wd_ref[lo:hi, :])
    o_ref[...] = _layer_norm(acc, g_ref[...], b_ref[...])


def _ffn_call(x, w_gate_up, w_down, g, b, *, alpha):
    n_tok, d_model = x.shape
    d_ff = w_down.shape[0]
    tm = min(FFN_TILE, n_tok)
    tok = pl.BlockSpec((tm, d_model), lambda t: (t, 0))
    return pl.pallas_call(
        functools.partial(_ffn_body, alpha=alpha, d_ff=d_ff),
        grid=(n_tok // tm,),
        in_specs=[tok, _resident(w_gate_up.shape), _resident(w_down.shape),
                  _resident(g.shape), _resident(b.shape)],
        out_specs=tok,
        out_shape=jax.ShapeDtypeStruct((n_tok, d_model), F32),
        compiler_params=pltpu.CompilerParams(
            dimension_semantics=("arbitrary",), vmem_limit_bytes=VMEM_LIMIT),
        name="swiglu",
    )(x, w_gate_up, w_down, g, b)


def _rope_tables(positions):
    inv_freq = ROPE_THETA ** (-jnp.arange(0, ROPE_DIM, 2, dtype=F32) / ROPE_DIM)
    ang = positions.astype(F32)[:, None, :] * inv_freq[None, :, None]
    return jnp.cos(ang), jnp.sin(ang)


def kernel(x, positions, w_in, w_sgu, b_sgu, sgu_ln_g, sgu_ln_b, lambda_q1, lambda_k1, lambda_q2,
           lambda_k2, subln_g, conv_w, w_branch, w_o, ln1_g, ln1_b, w_gate_up, w_down, ln2_g, ln2_b):
    bsz, s_len, d_model = x.shape
    depth = w_in.shape[0]
    half = d_model // 2
    alpha = (2 * depth) ** 0.25
    rope_cos, rope_sin = _rope_tables(positions)
    tril = jnp.tril(jnp.ones((CHUNK, CHUNK), dtype=bool))
    row2 = lambda a: a.reshape(1, -1)

    for l in range(depth):
        lambda_init = 0.8 - 0.6 * math.exp(-0.3 * l)
        lam = (jnp.exp(jnp.sum(lambda_q1[l] * lambda_k1[l])) - jnp.exp(jnp.sum(lambda_q2[l] * lambda_k2[l]))
               + lambda_init).reshape(1, 1).astype(F32)
        w_s = jnp.where(tril[None], w_sgu[l], 0.0).astype(BF16)
        b_s = jnp.repeat(b_sgu[l].T, half // SGU_GROUPS, axis=1)
        qT, k, vT, part, gb = _in_proj_call(
            x, rope_cos, rope_sin, w_in[l].astype(BF16), w_s, b_s, row2(sgu_ln_g[l]), row2(sgu_ln_b[l]),
            conv_w[l].T, w_branch[l, 0].astype(BF16), w_branch[l, 2].astype(BF16))
        x = _attn_call(lam, qT, k, vT, part, gb, x, subln_g[l].reshape(-1, 1),
                       w_branch[l, 1].astype(BF16), w_o[l].astype(BF16), row2(ln1_g[l]), row2(ln1_b[l]),
                       alpha=alpha, out_scale=1.0 - lambda_init)
        x = _ffn_call(x.reshape(bsz * s_len, d_model), w_gate_up[l].astype(BF16), w_down[l].astype(BF16),
                      row2(ln2_g[l]), row2(ln2_b[l]), alpha=alpha).reshape(bsz, s_len, d_model)
    return x
```

```python
import functools
import math

import jax
import jax.numpy as jnp
from jax import lax
from jax.experimental import pallas as pl
from jax.experimental.pallas import tpu as pltpu

F32 = jnp.float32
BF16 = jnp.bfloat16

CHUNK = 128
SGU_GROUPS = 4
DIFF_HEADS = 4
HEAD_DIM = 64
ROPE_DIM = HEAD_DIM // 4
ROPE_HALF = ROPE_DIM // 2
ROPE_THETA = 500000.0
CONV_K = 3
EPS = 1e-5
LOG2E = 1.4426950408889634
NEG_BIG = -1e30
SUM_ROWS = 16

PROJ_TILE = 512
ATTN_Q_BLOCK = 512
ATTN_K_BLOCK = 256
FFN_TILE = 512
VMEM_LIMIT = 56 * 1024 * 1024


def _dot(a, b):
    return jnp.dot(a, b, preferred_element_type=F32)


def _layer_norm(x, g, b):
    mu = jnp.mean(x, axis=-1, keepdims=True)
    xc = x - mu
    var = jnp.mean(xc * xc, axis=-1, keepdims=True)
    return xc * lax.rsqrt(var + EPS) * g + b


def _resident(shape):
    nd = len(shape)
    return pl.BlockSpec(shape, lambda *_: (0,) * nd, pipeline_mode=pl.Buffered(1))


def _in_proj_body(x_ref, cos_ref, sin_ref, w_in_ref, wsgu_ref, bsgu_ref, lng_ref, lnb_ref,
                  convw_ref, wb0_ref, wb2_ref,
                  qT_ref, k_ref, vT_ref, part_ref, gb_ref, carry_ref, *, d_model):
    tm = x_ref.shape[0]
    half = d_model // 2
    t = pl.program_id(1)
    xb = x_ref[...].astype(BF16)

    def proj(lo, hi):
        return _dot(xb, w_in_ref[:, lo:hi])

    o1 = d_model
    qk_w = 2 * DIFF_HEADS * HEAD_DIM
    v_w = DIFF_HEADS * 2 * HEAD_DIM
    o3 = o1 + 2 * qk_w
    o4 = o3 + v_w
    o5 = o4 + 3 * half

    za = proj(0, d_model)
    zq_t = proj(o1, o1 + qk_w).T
    zk_t = proj(o1 + qk_w, o3).T

    za = 0.5 * za * (1.0 + lax.erf(za * (1.0 / math.sqrt(2.0))))
    u = za[:, :half]
    v = _layer_norm(za[:, half:], lng_ref[...], lnb_ref[...]).astype(BF16)

    cos, sin = cos_ref[...], sin_ref[...]

    def rope_t(z_t):
        pieces = []
        for lo in range(0, qk_w, HEAD_DIM):
            r1, r2 = z_t[lo:lo + ROPE_HALF], z_t[lo + ROPE_HALF:lo + ROPE_DIM]
            pieces += [r1 * cos - r2 * sin, r2 * cos + r1 * sin, z_t[lo + ROPE_DIM:lo + HEAD_DIM]]
        return jnp.concatenate(pieces, axis=0)

    q_t = (rope_t(zq_t) * (HEAD_DIM ** -0.5 * LOG2E)).astype(BF16)
    no_q = jnp.zeros((HEAD_DIM, tm), BF16)
    maps = []
    for lo in range(0, qk_w, 2 * HEAD_DIM):
        maps += [q_t[lo:lo + HEAD_DIM], no_q, no_q, q_t[lo + HEAD_DIM:lo + 2 * HEAD_DIM]]
    qT_ref[...] = jnp.concatenate(maps, axis=0)
    k_ref[...] = rope_t(zk_t).T.astype(BF16)
    vT_ref[...] = proj(o3, o4).T.astype(BF16)
    gb_ref[...] = jax.nn.sigmoid(proj(o5 + d_model, o5 + 2 * d_model)).astype(BF16)

    gw = half // SGU_GROUPS
    rows = []
    for c in range(tm // CHUNK):
        cols = [_dot(wsgu_ref[g], v[c * CHUNK:(c + 1) * CHUNK, g * gw:(g + 1) * gw])
                for g in range(SGU_GROUPS)]
        rows.append(jnp.concatenate(cols, axis=1) + bsgu_ref[...])
    y_a = (u * jnp.concatenate(rows, axis=0)).astype(BF16)

    zc = proj(o4, o5)
    h = zc[:, half:2 * half] * zc[:, 2 * half:]

    @pl.when(t == 0)
    def _():
        carry_ref[...] = jnp.zeros_like(carry_ref)

    prev = carry_ref[...]
    row = lax.broadcasted_iota(jnp.int32, (tm, half), 0)
    h1 = jnp.where(row == 0, prev[7:8, :], pltpu.roll(h, 1, 0))
    h2 = jnp.where(row == 0, prev[6:7, :], jnp.where(row == 1, prev[7:8, :], pltpu.roll(h, 2, 0)))
    carry_ref[...] = h[tm - 8:, :]
    conv = h2 * convw_ref[0:1, :] + h1 * convw_ref[1:2, :] + h * convw_ref[2:3, :]
    y_c = (zc[:, :half] * conv).astype(BF16)

    g_a = jax.nn.sigmoid(proj(o5, o5 + d_model))
    g_c = jax.nn.sigmoid(proj(o5 + 2 * d_model, o5 + 3 * d_model))
    part_ref[...] = g_a * _dot(y_a, wb0_ref[...]) + g_c * _dot(y_c, wb2_ref[...])


def _in_proj_call(x, rope_cos, rope_sin, w_in, w_sgu, b_sgu, ln_g, ln_b, conv_w, wb0, wb2):
    bsz, s_len, d_model = x.shape
    tm = min(PROJ_TILE, s_len)
    half = d_model // 2
    qk_w = 2 * DIFF_HEADS * HEAD_DIM
    v_w = DIFF_HEADS * 2 * HEAD_DIM
    grid = (bsz, s_len // tm)
    tok = lambda w: pl.BlockSpec((None, tm, w), lambda b, t: (b, t, 0))
    tokT = lambda w: pl.BlockSpec((None, w, tm), lambda b, t: (b, 0, t))
    return pl.pallas_call(
        functools.partial(_in_proj_body, d_model=d_model),
        grid=grid,
        in_specs=[tok(d_model), tokT(ROPE_HALF), tokT(ROPE_HALF),
                  _resident(w_in.shape), _resident(w_sgu.shape), _resident(b_sgu.shape),
                  _resident(ln_g.shape), _resident(ln_b.shape), _resident(conv_w.shape),
                  _resident(wb0.shape), _resident(wb2.shape)],
        out_specs=[tokT(2 * qk_w), tok(qk_w), tokT(v_w), tok(d_model), tok(d_model)],
        out_shape=[jax.ShapeDtypeStruct((bsz, 2 * qk_w, s_len), BF16),
                   jax.ShapeDtypeStruct((bsz, s_len, qk_w), BF16),
                   jax.ShapeDtypeStruct((bsz, v_w, s_len), BF16),
                   jax.ShapeDtypeStruct((bsz, s_len, d_model), F32),
                   jax.ShapeDtypeStruct((bsz, s_len, d_model), BF16)],
        scratch_shapes=[pltpu.VMEM((8, half), F32)],
        compiler_params=pltpu.CompilerParams(
            dimension_semantics=("arbitrary", "arbitrary"), vmem_limit_bytes=VMEM_LIMIT),
        name="in_proj",
    )(x, rope_cos, rope_sin, w_in, w_sgu, b_sgu, ln_g, ln_b, conv_w, wb0, wb2)


def _attn_body(lam_ref, qm_ref, k_ref, vT_ref, part_ref, gb_ref, x_ref, subg_ref, wb1_ref,
               wo_ref, g1_ref, b1_ref, o_ref, sa_ref, sb_ref, acc_ref, m_ref, mba_ref, mbb_ref,
               *, alpha, out_scale):
    n_maps, bk, bq = sa_ref.shape
    assert bq == 2 * bk and n_maps == 2 * DIFF_HEADS
    half = bq // 2
    i = pl.program_id(1)
    n_off = 2 * i
    lam = lam_ref[0, 0]
    pair_w = 2 * HEAD_DIM
    visible = (lax.broadcasted_iota(jnp.int32, (bk, bq), 0)
               <= lax.broadcasted_iota(jnp.int32, (bk, bq), 1))

    def colmax(s):
        return jnp.max(s, axis=0, keepdims=True)

    def rows(r):
        return slice(r * pair_w, (r + 1) * pair_w)

    def keys(kb):
        return pl.ds(pl.multiple_of(kb * bk, bk), bk)

    def values(h, kb):
        return jnp.concatenate([vT_ref[rows(h), keys(kb)], jnp.ones((SUM_ROWS, bk), BF16)], axis=0)

    def scores(idx, kb, q_lo=0):
        return _dot(k_ref[keys(kb), rows(idx // 2)], qm_ref[rows(idx), q_lo:])

    def stage(s_ref, mb_ref, idx, kb):
        s = scores(idx, kb)
        s_ref[idx] = s
        mb_ref[idx] = colmax(s)

    def update(s, idx, kb, mb, q_lo=0):
        m_run = m_ref[idx, :, q_lo:]
        m_new = jnp.maximum(m_run, mb)
        p = jnp.exp2(s - m_new).astype(BF16)
        acc_ref[idx, :, q_lo:] = (jnp.exp2(m_run - m_new) * acc_ref[idx, :, q_lo:]
                                  + _dot(values(idx // 2, kb), p))
        m_ref[idx, :, q_lo:] = m_new

    for idx in range(n_maps):
        stage(sa_ref, mba_ref, idx, 0)
    acc_ref[...] = jnp.zeros_like(acc_ref)
    m_ref[...] = jnp.full_like(m_ref, NEG_BIG)

    def pair(j):
        for idx in range(n_maps):
            stage(sb_ref, mbb_ref, idx, j + 1)
            update(sa_ref[idx], idx, j, mba_ref[idx])
        for idx in range(n_maps):
            stage(sa_ref, mba_ref, idx, j + 2)
            update(sb_ref[idx], idx, j + 1, mbb_ref[idx])

    odd = i % 2

    @pl.when(odd == 1)
    def _():
        pair(0)

    @pl.loop(0, i // 2)
    def _(t):
        j = 2 * odd + 4 * t
        pair(j)
        pair(j + 2)

    for idx in range(n_maps):
        sb_ref[idx, :, half:] = jnp.where(visible[:, :half], scores(idx, n_off + 1, half), NEG_BIG)
        s = jnp.where(visible, sa_ref[idx], NEG_BIG)
        update(s, idx, n_off, colmax(s))
    heads = []
    for h in range(DIFF_HEADS):
        o_maps = []
        for idx in (2 * h, 2 * h + 1):
            s = sb_ref[idx, :, half:]
            update(s, idx, n_off + 1, colmax(s), half)
            o_maps.append(acc_ref[idx, :pair_w, :] * (1.0 / acc_ref[idx, pair_w:pair_w + 1, :]))
        o = o_maps[0] - lam * o_maps[1]
        ms = jnp.mean(o * o, axis=0, keepdims=True)
        heads.append(o * lax.rsqrt(ms + EPS) * subg_ref[...] * out_scale)

    y_bt = jnp.concatenate(heads, axis=0)
    for lo in (0, half):
        qs = slice(lo, lo + half)
        y_b = y_bt[:, qs].T.astype(BF16)
        merged = part_ref[qs, :] + gb_ref[qs, :].astype(F32) * _dot(y_b, wb1_ref[...])
        mix = _dot(merged.astype(BF16), wo_ref[...])
        o_ref[qs, :] = _layer_norm(alpha * x_ref[qs, :] + mix, g1_ref[...], b1_ref[...])


def _attn_call(lam, qT, k, vT, part, gb, x, subg, wb1, w_o, g1, b1, *, alpha, out_scale):
    bsz, s_len, d_model = x.shape
    bq, bk = ATTN_Q_BLOCK, ATTN_K_BLOCK
    assert bq == 2 * bk and s_len % bq == 0
    n_maps = 2 * DIFF_HEADS
    grid = (bsz, s_len // bq)
    tok = lambda w: pl.BlockSpec((None, bq, w), lambda b, i: (b, i, 0))
    per_batch = lambda shape: pl.BlockSpec((None,) + shape, lambda b, i: (b, 0, 0),
                                           pipeline_mode=pl.Buffered(1))
    return pl.pallas_call(
        functools.partial(_attn_body, alpha=alpha, out_scale=out_scale),
        grid=grid,
        in_specs=[pl.BlockSpec(memory_space=pltpu.SMEM),
                  pl.BlockSpec((None, qT.shape[1], bq), lambda b, i: (b, 0, i)),
                  per_batch(k.shape[1:]), per_batch(vT.shape[1:]),
                  tok(d_model), tok(d_model), tok(d_model),
                  _resident(subg.shape), _resident(wb1.shape), _resident(w_o.shape),
                  _resident(g1.shape), _resident(b1.shape)],
        out_specs=tok(d_model),
        out_shape=jax.ShapeDtypeStruct((bsz, s_len, d_model), F32),
        scratch_shapes=[pltpu.VMEM((n_maps, bk, bq), F32), pltpu.VMEM((n_maps, bk, bq), F32),
                        pltpu.VMEM((n_maps, 2 * HEAD_DIM + SUM_ROWS, bq), F32),
                        pltpu.VMEM((n_maps, 1, bq), F32), pltpu.VMEM((n_maps, 1, bq), F32),
                        pltpu.VMEM((n_maps, 1, bq), F32)],
        compiler_params=pltpu.CompilerParams(
            dimension_semantics=("arbitrary", "arbitrary"), vmem_limit_bytes=VMEM_LIMIT),
        name="diff_attn",
    )(lam, qT, k, vT, part, gb, x, subg, wb1, w_o, g1, b1)


def _ffn_chunks(d_ff):
    step = 1024
    return [(lo, min(lo + step, d_ff)) for lo in range(0, d_ff, step)]


def _ffn_body(x_ref, wgu_ref, wd_ref, g_ref, b_ref, o_ref, *, alpha, d_ff):
    x = x_ref[...]
    xb = x.astype(BF16)
    acc = alpha * x
    for lo, hi in _ffn_chunks(d_ff):
        gate = _dot(xb, wgu_ref[:, lo:hi])
        up = _dot(xb, wgu_ref[:, d_ff + lo:d_ff + hi])
        hidden = (gate * jax.nn.sigmoid(gate) * up).astype(BF16)
        acc = acc + _dot(hidden, wd_ref[lo:hi, :])
    o_ref[...] = _layer_norm(acc, g_ref[...], b_ref[...])


def _ffn_call(x, w_gate_up, w_down, g, b, *, alpha):
    n_tok, d_model = x.shape
    d_ff = w_down.shape[0]
    tm = min(FFN_TILE, n_tok)
    tok = pl.BlockSpec((tm, d_model), lambda t: (t, 0))
    return pl.pallas_call(
        functools.partial(_ffn_body, alpha=alpha, d_ff=d_ff),
        grid=(n_tok // tm,),
        in_specs=[tok, _resident(w_gate_up.shape), _resident(w_down.shape),
                  _resident(g.shape), _resident(b.shape)],
        out_specs=tok,
        out_shape=jax.ShapeDtypeStruct((n_tok, d_model), F32),
        compiler_params=pltpu.CompilerParams(
            dimension_semantics=("arbitrary",), vmem_limit_bytes=VMEM_LIMIT),
        name="swiglu",
    )(x, w_gate_up, w_down, g, b)


def _rope_tables(positions):
    inv_freq = ROPE_THETA ** (-jnp.arange(0, ROPE_DIM, 2, dtype=F32) / ROPE_DIM)
    ang = positions.astype(F32)[:, None, :] * inv_freq[None, :, None]
    return jnp.cos(ang), jnp.sin(ang)


def kernel(x, positions, w_in, w_sgu, b_sgu, sgu_ln_g, sgu_ln_b, lambda_q1, lambda_k1, lambda_q2,
           lambda_k2, subln_g, conv_w, w_branch, w_o, ln1_g, ln1_b, w_gate_up, w_down, ln2_g, ln2_b):
    bsz, s_len, d_model = x.shape
    depth = w_in.shape[0]
    half = d_model // 2
    alpha = (2 * depth) ** 0.25
    rope_cos, rope_sin = _rope_tables(positions)
    tril = jnp.tril(jnp.ones((CHUNK, CHUNK), dtype=bool))
    row2 = lambda a: a.reshape(1, -1)

    for l in range(depth):
        lambda_init = 0.8 - 0.6 * math.exp(-0.3 * l)
        lam = (jnp.exp(jnp.sum(lambda_q1[l] * lambda_k1[l])) - jnp.exp(jnp.sum(lambda_q2[l] * lambda_k2[l]))
               + lambda_init).reshape(1, 1).astype(F32)
        w_s = jnp.where(tril[None], w_sgu[l], 0.0).astype(BF16)
        b_s = jnp.repeat(b_sgu[l].T, half // SGU_GROUPS, axis=1)
        qT, k, vT, part, gb = _in_proj_call(
            x, rope_cos, rope_sin, w_in[l].astype(BF16), w_s, b_s, row2(sgu_ln_g[l]), row2(sgu_ln_b[l]),
            conv_w[l].T, w_branch[l, 0].astype(BF16), w_branch[l, 2].astype(BF16))
        x = _attn_call(lam, qT, k, vT, part, gb, x, subln_g[l].reshape(-1, 1),
                       w_branch[l, 1].astype(BF16), w_o[l].astype(BF16), row2(ln1_g[l]), row2(ln1_b[l]),
                       alpha=alpha, out_scale=1.0 - lambda_init)
        x = _ffn_call(x.reshape(bsz * s_len, d_model), w_gate_up[l].astype(BF16), w_down[l].astype(BF16),
                      row2(ln2_g[l]), row2(ln2_b[l]), alpha=alpha).reshape(bsz, s_len, d_model)
    return x
```

```python
import functools
import math

import jax
import jax.numpy as jnp
from jax import lax
from jax.experimental import pallas as pl
from jax.experimental.pallas import tpu as pltpu

F32 = jnp.float32
BF16 = jnp.bfloat16

CHUNK = 128
SGU_GROUPS = 4
DIFF_HEADS = 4
HEAD_DIM = 64
ROPE_DIM = HEAD_DIM // 4
ROPE_HALF = ROPE_DIM // 2
ROPE_THETA = 500000.0
CONV_K = 3
EPS = 1e-5
LOG2E = 1.4426950408889634
NEG_BIG = -1e30
SUM_ROWS = 16

PROJ_TILE = 512
ATTN_Q_BLOCK = 512
ATTN_K_BLOCK = 256
FFN_TILE = 512
VMEM_LIMIT = 56 * 1024 * 1024


def _dot(a, b):
    return jnp.dot(a, b, preferred_element_type=F32)


def _layer_norm(x, g, b):
    mu = jnp.mean(x, axis=-1, keepdims=True)
    xc = x - mu
    var = jnp.mean(xc * xc, axis=-1, keepdims=True)
    return xc * lax.rsqrt(var + EPS) * g + b


def _resident(shape):
    nd = len(shape)
    return pl.BlockSpec(shape, lambda *_: (0,) * nd, pipeline_mode=pl.Buffered(1))


def _in_proj_body(x_ref, cos_ref, sin_ref, w_in_ref, wsgu_ref, bsgu_ref, lng_ref, lnb_ref,
                  convw_ref, wb0_ref, wb2_ref,
                  qT_ref, k_ref, vT_ref, part_ref, gb_ref, carry_ref, *, d_model):
    tm = x_ref.shape[0]
    half = d_model // 2
    t = pl.program_id(1)
    xb = x_ref[...].astype(BF16)

    def proj(lo, hi):
        return _dot(xb, w_in_ref[:, lo:hi])

    o1 = d_model
    qk_w = 2 * DIFF_HEADS * HEAD_DIM
    v_w = DIFF_HEADS * 2 * HEAD_DIM
    o3 = o1 + 2 * qk_w
    o4 = o3 + v_w
    o5 = o4 + 3 * half

    za = proj(0, d_model)
    zq_t = proj(o1, o1 + qk_w).T
    zk_t = proj(o1 + qk_w, o3).T

    za = 0.5 * za * (1.0 + lax.erf(za * (1.0 / math.sqrt(2.0))))
    u = za[:, :half]
    v = _layer_norm(za[:, half:], lng_ref[...], lnb_ref[...]).astype(BF16)

    cos, sin = cos_ref[...], sin_ref[...]

    def rope_t(z_t):
        pieces = []
        for lo in range(0, qk_w, HEAD_DIM):
            r1, r2 = z_t[lo:lo + ROPE_HALF], z_t[lo + ROPE_HALF:lo + ROPE_DIM]
            pieces += [r1 * cos - r2 * sin, r2 * cos + r1 * sin, z_t[lo + ROPE_DIM:lo + HEAD_DIM]]
        return jnp.concatenate(pieces, axis=0)

    q_t = (rope_t(zq_t) * (HEAD_DIM ** -0.5 * LOG2E)).astype(BF16)
    no_q = jnp.zeros((HEAD_DIM, tm), BF16)
    maps = []
    for lo in range(0, qk_w, 2 * HEAD_DIM):
        maps += [q_t[lo:lo + HEAD_DIM], no_q, no_q, q_t[lo + HEAD_DIM:lo + 2 * HEAD_DIM]]
    qT_ref[...] = jnp.concatenate(maps, axis=0)
    k_ref[...] = rope_t(zk_t).T.astype(BF16)
    vT_ref[...] = proj(o3, o4).T.astype(BF16)
    gb_ref[...] = jax.nn.sigmoid(proj(o5 + d_model, o5 + 2 * d_model)).astype(BF16)

    gw = half // SGU_GROUPS
    rows = []
    for c in range(tm // CHUNK):
        cols = [_dot(wsgu_ref[g], v[c * CHUNK:(c + 1) * CHUNK, g * gw:(g + 1) * gw])
                for g in range(SGU_GROUPS)]
        rows.append(jnp.concatenate(cols, axis=1) + bsgu_ref[...])
    y_a = (u * jnp.concatenate(rows, axis=0)).astype(BF16)

    zc = proj(o4, o5)
    h = zc[:, half:2 * half] * zc[:, 2 * half:]

    @pl.when(t == 0)
    def _():
        carry_ref[...] = jnp.zeros_like(carry_ref)

    prev = carry_ref[...]
    row = lax.broadcasted_iota(jnp.int32, (tm, half), 0)
    h1 = jnp.where(row == 0, prev[7:8, :], pltpu.roll(h, 1, 0))
    h2 = jnp.where(row == 0, prev[6:7, :], jnp.where(row == 1, prev[7:8, :], pltpu.roll(h, 2, 0)))
    carry_ref[...] = h[tm - 8:, :]
    conv = h2 * convw_ref[0:1, :] + h1 * convw_ref[1:2, :] + h * convw_ref[2:3, :]
    y_c = (zc[:, :half] * conv).astype(BF16)

    g_a = jax.nn.sigmoid(proj(o5, o5 + d_model))
    g_c = jax.nn.sigmoid(proj(o5 + 2 * d_model, o5 + 3 * d_model))
    part_ref[...] = g_a * _dot(y_a, wb0_ref[...]) + g_c * _dot(y_c, wb2_ref[...])


def _in_proj_call(x, rope_cos, rope_sin, w_in, w_sgu, b_sgu, ln_g, ln_b, conv_w, wb0, wb2):
    bsz, s_len, d_model = x.shape
    tm = min(PROJ_TILE, s_len)
    half = d_model // 2
    qk_w = 2 * DIFF_HEADS * HEAD_DIM
    v_w = DIFF_HEADS * 2 * HEAD_DIM
    grid = (bsz, s_len // tm)
    tok = lambda w: pl.BlockSpec((None, tm, w), lambda b, t: (b, t, 0))
    tokT = lambda w: pl.BlockSpec((None, w, tm), lambda b, t: (b, 0, t))
    return pl.pallas_call(
        functools.partial(_in_proj_body, d_model=d_model),
        grid=grid,
        in_specs=[tok(d_model), tokT(ROPE_HALF), tokT(ROPE_HALF),
                  _resident(w_in.shape), _resident(w_sgu.shape), _resident(b_sgu.shape),
                  _resident(ln_g.shape), _resident(ln_b.shape), _resident(conv_w.shape),
                  _resident(wb0.shape), _resident(wb2.shape)],
        out_specs=[tokT(2 * qk_w), tok(qk_w), tokT(v_w), tok(d_model), tok(d_model)],
        out_shape=[jax.ShapeDtypeStruct((bsz, 2 * qk_w, s_len), BF16),
                   jax.ShapeDtypeStruct((bsz, s_len, qk_w), BF16),
                   jax.ShapeDtypeStruct((bsz, v_w, s_len), BF16),
                   jax.ShapeDtypeStruct((bsz, s_len, d_model), F32),
                   jax.ShapeDtypeStruct((bsz, s_len, d_model), BF16)],
        scratch_shapes=[pltpu.VMEM((8, half), F32)],
        compiler_params=pltpu.CompilerParams(
            dimension_semantics=("arbitrary", "arbitrary"), vmem_limit_bytes=VMEM_LIMIT),
        name="in_proj",
    )(x, rope_cos, rope_sin, w_in, w_sgu, b_sgu, ln_g, ln_b, conv_w, wb0, wb2)


def _attn_body(lam_ref, qm_ref, k_ref, vT_ref, subg_ref, o_ref, sa_ref, sb_ref, acc_ref, m_ref,
               mba_ref, mbb_ref, *, out_scale):
    n_maps, bk, bq = sa_ref.shape
    assert bq == 2 * bk and n_maps == 2 * DIFF_HEADS
    half = bq // 2
    i = pl.program_id(1)
    n_off = 2 * i
    lam = lam_ref[0, 0]
    pair_w = 2 * HEAD_DIM
    visible = (lax.broadcasted_iota(jnp.int32, (bk, bq), 0)
               <= lax.broadcasted_iota(jnp.int32, (bk, bq), 1))

    def colmax(s):
        return jnp.max(s, axis=0, keepdims=True)

    def rows(r):
        return slice(r * pair_w, (r + 1) * pair_w)

    def keys(kb):
        return pl.ds(pl.multiple_of(kb * bk, bk), bk)

    def values(h, kb):
        return jnp.concatenate([vT_ref[rows(h), keys(kb)], jnp.ones((SUM_ROWS, bk), BF16)], axis=0)

    def scores(idx, kb, q_lo=0):
        return _dot(k_ref[keys(kb), rows(idx // 2)], qm_ref[rows(idx), q_lo:])

    def stage(s_ref, mb_ref, idx, kb):
        s = scores(idx, kb)
        s_ref[idx] = s
        mb_ref[idx] = colmax(s)

    def update(s, idx, kb, mb, q_lo=0):
        m_run = m_ref[idx, :, q_lo:]
        m_new = jnp.maximum(m_run, mb)
        p = jnp.exp2(s - m_new).astype(BF16)
        acc_ref[idx, :, q_lo:] = (jnp.exp2(m_run - m_new) * acc_ref[idx, :, q_lo:]
                                  + _dot(values(idx // 2, kb), p))
        m_ref[idx, :, q_lo:] = m_new

    for idx in range(n_maps):
        stage(sa_ref, mba_ref, idx, 0)
    acc_ref[...] = jnp.zeros_like(acc_ref)
    m_ref[...] = jnp.full_like(m_ref, NEG_BIG)

    def pair(j):
        for idx in range(n_maps):
            stage(sb_ref, mbb_ref, idx, j + 1)
            update(sa_ref[idx], idx, j, mba_ref[idx])
        for idx in range(n_maps):
            stage(sa_ref, mba_ref, idx, j + 2)
            update(sb_ref[idx], idx, j + 1, mbb_ref[idx])

    odd = i % 2

    @pl.when(odd == 1)
    def _():
        pair(0)

    @pl.loop(0, i // 2)
    def _(t):
        j = 2 * odd + 4 * t
        pair(j)
        pair(j + 2)

    for idx in range(n_maps):
        sb_ref[idx, :, half:] = jnp.where(visible[:, :half], scores(idx, n_off + 1, half), NEG_BIG)
        s = jnp.where(visible, sa_ref[idx], NEG_BIG)
        update(s, idx, n_off, colmax(s))
    for h in range(DIFF_HEADS):
        o_maps = []
        for idx in (2 * h, 2 * h + 1):
            s = sb_ref[idx, :, half:]
            update(s, idx, n_off + 1, colmax(s), half)
            o_maps.append(acc_ref[idx, :pair_w, :] * (1.0 / acc_ref[idx, pair_w:pair_w + 1, :]))
        o = o_maps[0] - lam * o_maps[1]
        ms = jnp.mean(o * o, axis=0, keepdims=True)
        o_ref[rows(h), :] = o * lax.rsqrt(ms + EPS) * subg_ref[...] * out_scale


def _attn_call(lam, qm, k, vT, subg, *, out_scale):
    bsz, s_len, _ = k.shape
    bq, bk = ATTN_Q_BLOCK, ATTN_K_BLOCK
    assert bq == 2 * bk and s_len % bq == 0
    n_maps = 2 * DIFF_HEADS
    grid = (bsz, s_len // bq)
    per_batch = lambda shape: pl.BlockSpec((None,) + shape, lambda b, i: (b, 0, 0))
    return pl.pallas_call(
        functools.partial(_attn_body, out_scale=out_scale),
        grid=grid,
        in_specs=[pl.BlockSpec(memory_space=pltpu.SMEM),
                  pl.BlockSpec((None, qm.shape[1], bq), lambda b, i: (b, 0, i)),
                  per_batch(k.shape[1:]), per_batch(vT.shape[1:]), _resident(subg.shape)],
        out_specs=pl.BlockSpec((None, vT.shape[1], bq), lambda b, i: (b, 0, i)),
        out_shape=jax.ShapeDtypeStruct((bsz, vT.shape[1], s_len), F32),
        scratch_shapes=[pltpu.VMEM((n_maps, bk, bq), F32), pltpu.VMEM((n_maps, bk, bq), F32),
                        pltpu.VMEM((n_maps, 2 * HEAD_DIM + SUM_ROWS, bq), F32),
                        pltpu.VMEM((n_maps, 1, bq), F32), pltpu.VMEM((n_maps, 1, bq), F32),
                        pltpu.VMEM((n_maps, 1, bq), F32)],
        compiler_params=pltpu.CompilerParams(
            dimension_semantics=("arbitrary", "arbitrary"), vmem_limit_bytes=VMEM_LIMIT),
        name="diff_attn",
    )(lam, qm, k, vT, subg)


def _ffn_chunks(d_ff):
    step = 1024
    return [(lo, min(lo + step, d_ff)) for lo in range(0, d_ff, step)]


def _ffn_body(yT_ref, part_ref, gb_ref, x_ref, wb1_ref, wo_ref, g1_ref, b1_ref, wgu_ref, wd_ref,
              g2_ref, b2_ref, o_ref, *, alpha, d_ff):
    tm = x_ref.shape[0]
    halves = [slice(lo, lo + tm // 2) for lo in (0, tm // 2)]
    y_b = yT_ref[...].T.astype(BF16)

    x1 = []
    for rs in halves:
        merged = part_ref[rs, :] + gb_ref[rs, :].astype(F32) * _dot(y_b[rs, :], wb1_ref[...])
        mix = _dot(merged.astype(BF16), wo_ref[...])
        x1.append(_layer_norm(alpha * x_ref[rs, :] + mix, g1_ref[...], b1_ref[...]))
    for rs, x in zip(halves, x1):
        xb = x.astype(BF16)
        acc = alpha * x
        for lo, hi in _ffn_chunks(d_ff):
            gate = _dot(xb, wgu_ref[:, lo:hi])
            up = _dot(xb, wgu_ref[:, d_ff + lo:d_ff + hi])
            hidden = (gate * jax.nn.sigmoid(gate) * up).astype(BF16)
            acc = acc + _dot(hidden, wd_ref[lo:hi, :])
        o_ref[rs, :] = _layer_norm(acc, g2_ref[...], b2_ref[...])


def _ffn_call(yT, part, gb, x, wb1, w_o, g1, b1, w_gate_up, w_down, g2, b2, *, alpha):
    bsz, s_len, d_model = x.shape
    d_ff = w_down.shape[0]
    tm = min(FFN_TILE, s_len)
    tok = lambda w: pl.BlockSpec((None, tm, w), lambda b, t: (b, t, 0))
    weights = (wb1, w_o, g1, b1, w_gate_up, w_down, g2, b2)
    return pl.pallas_call(
        functools.partial(_ffn_body, alpha=alpha, d_ff=d_ff),
        grid=(bsz, s_len // tm),
        in_specs=[pl.BlockSpec((None, yT.shape[1], tm), lambda b, t: (b, 0, t)),
                  tok(d_model), tok(d_model), tok(d_model)] + [_resident(w.shape) for w in weights],
        out_specs=tok(d_model),
        out_shape=jax.ShapeDtypeStruct((bsz, s_len, d_model), F32),
        compiler_params=pltpu.CompilerParams(
            dimension_semantics=("arbitrary", "arbitrary"), vmem_limit_bytes=VMEM_LIMIT),
        name="merge_swiglu",
    )(yT, part, gb, x, *weights)


def _rope_tables(positions):
    inv_freq = ROPE_THETA ** (-jnp.arange(0, ROPE_DIM, 2, dtype=F32) / ROPE_DIM)
    ang = positions.astype(F32)[:, None, :] * inv_freq[None, :, None]
    return jnp.cos(ang), jnp.sin(ang)


def kernel(x, positions, w_in, w_sgu, b_sgu, sgu_ln_g, sgu_ln_b, lambda_q1, lambda_k1, lambda_q2,
           lambda_k2, subln_g, conv_w, w_branch, w_o, ln1_g, ln1_b, w_gate_up, w_down, ln2_g, ln2_b):
    bsz, s_len, d_model = x.shape
    depth = w_in.shape[0]
    half = d_model // 2
    alpha = (2 * depth) ** 0.25
    rope_cos, rope_sin = _rope_tables(positions)
    tril = jnp.tril(jnp.ones((CHUNK, CHUNK), dtype=bool))
    row2 = lambda a: a.reshape(1, -1)

    for l in range(depth):
        lambda_init = 0.8 - 0.6 * math.exp(-0.3 * l)
        lam = (jnp.exp(jnp.sum(lambda_q1[l] * lambda_k1[l])) - jnp.exp(jnp.sum(lambda_q2[l] * lambda_k2[l]))
               + lambda_init).reshape(1, 1).astype(F32)
        w_s = jnp.where(tril[None], w_sgu[l], 0.0).astype(BF16)
        b_s = jnp.repeat(b_sgu[l].T, half // SGU_GROUPS, axis=1)
        qT, k, vT, part, gb = _in_proj_call(
            x, rope_cos, rope_sin, w_in[l].astype(BF16), w_s, b_s, row2(sgu_ln_g[l]), row2(sgu_ln_b[l]),
            conv_w[l].T, w_branch[l, 0].astype(BF16), w_branch[l, 2].astype(BF16))
        yT = _attn_call(lam, qT, k, vT, subln_g[l].reshape(-1, 1), out_scale=1.0 - lambda_init)
        x = _ffn_call(yT, part, gb, x, w_branch[l, 1].astype(BF16), w_o[l].astype(BF16),
                      row2(ln1_g[l]), row2(ln1_b[l]), w_gate_up[l].astype(BF16), w_down[l].astype(BF16),
                      row2(ln2_g[l]), row2(ln2_b[l]), alpha=alpha)
    return x
```

```python
import functools
import math

import jax
import jax.numpy as jnp
from jax import lax
from jax.experimental import pallas as pl
from jax.experimental.pallas import tpu as pltpu

F32 = jnp.float32
BF16 = jnp.bfloat16

CHUNK = 128
SGU_GROUPS = 4
DIFF_HEADS = 4
HEAD_DIM = 64
ROPE_DIM = HEAD_DIM // 4
ROPE_HALF = ROPE_DIM // 2
ROPE_THETA = 500000.0
CONV_K = 3
EPS = 1e-5
LOG2E = 1.4426950408889634
NEG_BIG = -1e30
SUM_ROWS = 16

PROJ_TILE = 1024
ATTN_Q_BLOCK = 512
ATTN_K_BLOCK = 256
FFN_TILE = 512
VMEM_LIMIT = 56 * 1024 * 1024


def _dot(a, b):
    return jnp.dot(a, b, preferred_element_type=F32)


def _layer_norm(x, g, b):
    mu = jnp.mean(x, axis=-1, keepdims=True)
    xc = x - mu
    var = jnp.mean(xc * xc, axis=-1, keepdims=True)
    return xc * lax.rsqrt(var + EPS) * g + b


def _resident(shape):
    nd = len(shape)
    return pl.BlockSpec(shape, lambda *_: (0,) * nd, pipeline_mode=pl.Buffered(1))


def _in_proj_body(x_ref, cos_ref, sin_ref, w_in_ref, wsgu_ref, bsgu_ref, lng_ref, lnb_ref,
                  convw_ref, wb0_ref, wb2_ref,
                  qT_ref, k_ref, vT_ref, part_ref, gb_ref, carry_ref, *, d_model):
    tm = x_ref.shape[0]
    half = d_model // 2
    t = pl.program_id(1)
    xb = x_ref[...].astype(BF16)

    def proj(lo, hi):
        return _dot(xb, w_in_ref[:, lo:hi])

    o1 = d_model
    qk_w = 2 * DIFF_HEADS * HEAD_DIM
    v_w = DIFF_HEADS * 2 * HEAD_DIM
    o3 = o1 + 2 * qk_w
    o4 = o3 + v_w
    o5 = o4 + 3 * half

    za = proj(0, d_model)
    zq_t = proj(o1, o1 + qk_w).T
    zk_t = proj(o1 + qk_w, o3).T

    za = 0.5 * za * (1.0 + lax.erf(za * (1.0 / math.sqrt(2.0))))
    u = za[:, :half]
    v = _layer_norm(za[:, half:], lng_ref[...], lnb_ref[...]).astype(BF16)

    cos, sin = cos_ref[...], sin_ref[...]

    def rope_t(z_t):
        pieces = []
        for lo in range(0, qk_w, HEAD_DIM):
            r1, r2 = z_t[lo:lo + ROPE_HALF], z_t[lo + ROPE_HALF:lo + ROPE_DIM]
            pieces += [r1 * cos - r2 * sin, r2 * cos + r1 * sin, z_t[lo + ROPE_DIM:lo + HEAD_DIM]]
        return jnp.concatenate(pieces, axis=0)

    q_t = (rope_t(zq_t) * (HEAD_DIM ** -0.5 * LOG2E)).astype(BF16)
    no_q = jnp.zeros((HEAD_DIM, tm), BF16)
    maps = []
    for lo in range(0, qk_w, 2 * HEAD_DIM):
        maps += [q_t[lo:lo + HEAD_DIM], no_q, no_q, q_t[lo + HEAD_DIM:lo + 2 * HEAD_DIM]]
    qT_ref[...] = jnp.concatenate(maps, axis=0)
    k_ref[...] = rope_t(zk_t).T.astype(BF16)
    vT_ref[...] = proj(o3, o4).T.astype(BF16)
    gb_ref[...] = jax.nn.sigmoid(proj(o5 + d_model, o5 + 2 * d_model)).astype(BF16)

    gw = half // SGU_GROUPS
    rows = []
    for c in range(tm // CHUNK):
        cols = [_dot(wsgu_ref[g], v[c * CHUNK:(c + 1) * CHUNK, g * gw:(g + 1) * gw])
                for g in range(SGU_GROUPS)]
        rows.append(jnp.concatenate(cols, axis=1) + bsgu_ref[...])
    y_a = (u * jnp.concatenate(rows, axis=0)).astype(BF16)

    zc = proj(o4, o5)
    h = zc[:, half:2 * half] * zc[:, 2 * half:]

    @pl.when(t == 0)
    def _():
        carry_ref[...] = jnp.zeros_like(carry_ref)

    prev = carry_ref[...]
    row = lax.broadcasted_iota(jnp.int32, (tm, half), 0)
    h1 = jnp.where(row == 0, prev[7:8, :], pltpu.roll(h, 1, 0))
    h2 = jnp.where(row == 0, prev[6:7, :], jnp.where(row == 1, prev[7:8, :], pltpu.roll(h, 2, 0)))
    carry_ref[...] = h[tm - 8:, :]
    conv = h2 * convw_ref[0:1, :] + h1 * convw_ref[1:2, :] + h * convw_ref[2:3, :]
    y_c = (zc[:, :half] * conv).astype(BF16)

    g_a = jax.nn.sigmoid(proj(o5, o5 + d_model))
    g_c = jax.nn.sigmoid(proj(o5 + 2 * d_model, o5 + 3 * d_model))
    part_ref[...] = (g_a * _dot(y_a, wb0_ref[...]) + g_c * _dot(y_c, wb2_ref[...])).astype(BF16)


def _in_proj_call(x, rope_cos, rope_sin, w_in, w_sgu, b_sgu, ln_g, ln_b, conv_w, wb0, wb2):
    bsz, s_len, d_model = x.shape
    tm = min(PROJ_TILE, s_len)
    half = d_model // 2
    qk_w = 2 * DIFF_HEADS * HEAD_DIM
    v_w = DIFF_HEADS * 2 * HEAD_DIM
    grid = (bsz, s_len // tm)
    tok = lambda w: pl.BlockSpec((None, tm, w), lambda b, t: (b, t, 0))
    tokT = lambda w: pl.BlockSpec((None, w, tm), lambda b, t: (b, 0, t))
    return pl.pallas_call(
        functools.partial(_in_proj_body, d_model=d_model),
        grid=grid,
        in_specs=[tok(d_model), tokT(ROPE_HALF), tokT(ROPE_HALF),
                  _resident(w_in.shape), _resident(w_sgu.shape), _resident(b_sgu.shape),
                  _resident(ln_g.shape), _resident(ln_b.shape), _resident(conv_w.shape),
                  _resident(wb0.shape), _resident(wb2.shape)],
        out_specs=[tokT(2 * qk_w), tok(qk_w), tokT(v_w), tok(d_model), tok(d_model)],
        out_shape=[jax.ShapeDtypeStruct((bsz, 2 * qk_w, s_len), BF16),
                   jax.ShapeDtypeStruct((bsz, s_len, qk_w), BF16),
                   jax.ShapeDtypeStruct((bsz, v_w, s_len), BF16),
                   jax.ShapeDtypeStruct((bsz, s_len, d_model), BF16),
                   jax.ShapeDtypeStruct((bsz, s_len, d_model), BF16)],
        scratch_shapes=[pltpu.VMEM((8, half), F32)],
        compiler_params=pltpu.CompilerParams(
            dimension_semantics=("arbitrary", "arbitrary"), vmem_limit_bytes=VMEM_LIMIT),
        name="in_proj",
    )(x, rope_cos, rope_sin, w_in, w_sgu, b_sgu, ln_g, ln_b, conv_w, wb0, wb2)


def _attn_body(lam_ref, qm_ref, qn_ref, k_ref, vT_ref, subg_ref, o_ref, sa_ref, sb_ref, acc_ref, m_ref,
               mba_ref, mbb_ref, *, out_scale):
    n_maps, bk, bq = sa_ref.shape
    assert bq == 2 * bk and n_maps == 2 * DIFF_HEADS
    half = bq // 2
    i = pl.program_id(1)
    n_off = 2 * i
    lam = lam_ref[0, 0]
    pair_w = 2 * HEAD_DIM
    visible = (lax.broadcasted_iota(jnp.int32, (bk, bq), 0)
               <= lax.broadcasted_iota(jnp.int32, (bk, bq), 1))

    def colmax(s):
        return jnp.max(s, axis=0, keepdims=True)

    def rows(r):
        return slice(r * pair_w, (r + 1) * pair_w)

    def keys(kb):
        return pl.ds(pl.multiple_of(kb * bk, bk), bk)

    def values(h, kb):
        return jnp.concatenate([vT_ref[rows(h), keys(kb)], jnp.ones((SUM_ROWS, bk), BF16)], axis=0)

    def scores(idx, kb, q_lo=0, q_ref=qm_ref):
        return _dot(k_ref[keys(kb), rows(idx // 2)], q_ref[rows(idx), q_lo:])

    def stage(s_ref, mb_ref, idx, kb, q_ref=qm_ref):
        s = scores(idx, kb, 0, q_ref)
        s_ref[idx] = s
        mb_ref[idx] = colmax(s)

    def update(s, idx, kb, mb, q_lo=0):
        m_run = m_ref[idx, :, q_lo:]
        m_new = jnp.maximum(m_run, mb)
        p = jnp.exp2(s - m_new).astype(BF16)
        acc_ref[idx, :, q_lo:] = (jnp.exp2(m_run - m_new) * acc_ref[idx, :, q_lo:]
                                  + _dot(values(idx // 2, kb), p))
        m_ref[idx, :, q_lo:] = m_new

    @pl.when(i == 0)
    def _():
        for idx in range(n_maps):
            stage(sa_ref, mba_ref, idx, 0)

    acc_ref[...] = jnp.zeros_like(acc_ref)
    m_ref[...] = jnp.full_like(m_ref, NEG_BIG)

    def pair(j):
        for idx in range(n_maps):
            stage(sb_ref, mbb_ref, idx, j + 1)
            update(sa_ref[idx], idx, j, mba_ref[idx])
        for idx in range(n_maps):
            stage(sa_ref, mba_ref, idx, j + 2)
            update(sb_ref[idx], idx, j + 1, mbb_ref[idx])

    odd = i % 2

    @pl.when(odd == 1)
    def _():
        pair(0)

    @pl.loop(0, i // 2)
    def _(t):
        j = 2 * odd + 4 * t
        pair(j)
        pair(j + 2)

    for idx in range(n_maps):
        sb_ref[idx, :, half:] = jnp.where(visible[:, :half], scores(idx, n_off + 1, half), NEG_BIG)
        s = jnp.where(visible, sa_ref[idx], NEG_BIG)
        update(s, idx, n_off, colmax(s))
        stage(sa_ref, mba_ref, idx, 0, qn_ref)
    for h in range(DIFF_HEADS):
        o_maps = []
        for idx in (2 * h, 2 * h + 1):
            s = sb_ref[idx, :, half:]
            update(s, idx, n_off + 1, colmax(s), half)
            o_maps.append(acc_ref[idx, :pair_w, :] * (1.0 / acc_ref[idx, pair_w:pair_w + 1, :]))
        o = o_maps[0] - lam * o_maps[1]
        ms = jnp.mean(o * o, axis=0, keepdims=True)
        o_ref[rows(h), :] = o * lax.rsqrt(ms + EPS) * subg_ref[...] * out_scale


def _attn_call(lam, qm, k, vT, subg, *, out_scale):
    bsz, s_len, _ = k.shape
    bq, bk = ATTN_Q_BLOCK, ATTN_K_BLOCK
    assert bq == 2 * bk and s_len % bq == 0
    n_maps = 2 * DIFF_HEADS
    n_q = s_len // bq
    grid = (bsz, n_q)
    per_batch = lambda shape: pl.BlockSpec((None,) + shape, lambda b, i: (b, 0, 0))
    return pl.pallas_call(
        functools.partial(_attn_body, out_scale=out_scale),
        grid=grid,
        in_specs=[pl.BlockSpec(memory_space=pltpu.SMEM),
                  pl.BlockSpec((None, qm.shape[1], bq), lambda b, i: (b, 0, i)),
                  pl.BlockSpec((None, qm.shape[1], bq), lambda b, i: (b, 0, jnp.minimum(i + 1, n_q - 1))),
                  per_batch(k.shape[1:]), per_batch(vT.shape[1:]), _resident(subg.shape)],
        out_specs=pl.BlockSpec((None, vT.shape[1], bq), lambda b, i: (b, 0, i)),
        out_shape=jax.ShapeDtypeStruct((bsz, vT.shape[1], s_len), F32),
        scratch_shapes=[pltpu.VMEM((n_maps, bk, bq), F32), pltpu.VMEM((n_maps, bk, bq), F32),
                        pltpu.VMEM((n_maps, 2 * HEAD_DIM + SUM_ROWS, bq), F32),
                        pltpu.VMEM((n_maps, 1, bq), F32), pltpu.VMEM((n_maps, 1, bq), F32),
                        pltpu.VMEM((n_maps, 1, bq), F32)],
        compiler_params=pltpu.CompilerParams(
            dimension_semantics=("arbitrary", "arbitrary"), vmem_limit_bytes=VMEM_LIMIT),
        name="diff_attn",
    )(lam, qm, qm, k, vT, subg)


def _ffn_chunks(d_ff):
    step = 1024
    return [(lo, min(lo + step, d_ff)) for lo in range(0, d_ff, step)]


def _ffn_body(yT_ref, part_ref, gb_ref, x_ref, wb1_ref, wo_ref, g1_ref, b1_ref, wgu_ref, wd_ref,
              g2_ref, b2_ref, o_ref, *, alpha, d_ff):
    tm = x_ref.shape[0]
    halves = [slice(lo, lo + tm // 2) for lo in (0, tm // 2)]
    y_b = yT_ref[...].T.astype(BF16)

    x1 = []
    for rs in halves:
        merged = part_ref[rs, :].astype(F32) + gb_ref[rs, :].astype(F32) * _dot(y_b[rs, :], wb1_ref[...])
        mix = _dot(merged.astype(BF16), wo_ref[...])
        x1.append(_layer_norm(alpha * x_ref[rs, :] + mix, g1_ref[...], b1_ref[...]))
    for rs, x in zip(halves, x1):
        xb = x.astype(BF16)
        acc = alpha * x
        for lo, hi in _ffn_chunks(d_ff):
            gate = _dot(xb, wgu_ref[:, lo:hi])
            up = _dot(xb, wgu_ref[:, d_ff + lo:d_ff + hi])
            hidden = (gate * jax.nn.sigmoid(gate) * up).astype(BF16)
            acc = acc + _dot(hidden, wd_ref[lo:hi, :])
        o_ref[rs, :] = _layer_norm(acc, g2_ref[...], b2_ref[...])


def _ffn_call(yT, part, gb, x, wb1, w_o, g1, b1, w_gate_up, w_down, g2, b2, *, alpha):
    bsz, s_len, d_model = x.shape
    d_ff = w_down.shape[0]
    tm = min(FFN_TILE, s_len)
    tok = lambda w: pl.BlockSpec((None, tm, w), lambda b, t: (b, t, 0))
    weights = (wb1, w_o, g1, b1, w_gate_up, w_down, g2, b2)
    return pl.pallas_call(
        functools.partial(_ffn_body, alpha=alpha, d_ff=d_ff),
        grid=(bsz, s_len // tm),
        in_specs=[pl.BlockSpec((None, yT.shape[1], tm), lambda b, t: (b, 0, t)),
                  tok(d_model), tok(d_model), tok(d_model)] + [_resident(w.shape) for w in weights],
        out_specs=tok(d_model),
        out_shape=jax.ShapeDtypeStruct((bsz, s_len, d_model), F32),
        compiler_params=pltpu.CompilerParams(
            dimension_semantics=("arbitrary", "arbitrary"), vmem_limit_bytes=VMEM_LIMIT),
        name="merge_swiglu",
    )(yT, part, gb, x, *weights)


def _rope_tables(positions):
    inv_freq = ROPE_THETA ** (-jnp.arange(0, ROPE_DIM, 2, dtype=F32) / ROPE_DIM)
    ang = positions.astype(F32)[:, None, :] * inv_freq[None, :, None]
    return jnp.cos(ang), jnp.sin(ang)


def kernel(x, positions, w_in, w_sgu, b_sgu, sgu_ln_g, sgu_ln_b, lambda_q1, lambda_k1, lambda_q2,
           lambda_k2, subln_g, conv_w, w_branch, w_o, ln1_g, ln1_b, w_gate_up, w_down, ln2_g, ln2_b):
    bsz, s_len, d_model = x.shape
    depth = w_in.shape[0]
    half = d_model // 2
    alpha = (2 * depth) ** 0.25
    rope_cos, rope_sin = _rope_tables(positions)
    tril = jnp.tril(jnp.ones((CHUNK, CHUNK), dtype=bool))
    row2 = lambda a: a.reshape(1, -1)

    for l in range(depth):
        lambda_init = 0.8 - 0.6 * math.exp(-0.3 * l)
        lam = (jnp.exp(jnp.sum(lambda_q1[l] * lambda_k1[l])) - jnp.exp(jnp.sum(lambda_q2[l] * lambda_k2[l]))
               + lambda_init).reshape(1, 1).astype(F32)
        w_s = jnp.where(tril[None], w_sgu[l], 0.0).astype(BF16)
        b_s = jnp.repeat(b_sgu[l].T, half // SGU_GROUPS, axis=1)
        qT, k, vT, part, gb = _in_proj_call(
            x, rope_cos, rope_sin, w_in[l].astype(BF16), w_s, b_s, row2(sgu_ln_g[l]), row2(sgu_ln_b[l]),
            conv_w[l].T, w_branch[l, 0].astype(BF16), w_branch[l, 2].astype(BF16))
        yT = _attn_call(lam, qT, k, vT, subln_g[l].reshape(-1, 1), out_scale=1.0 - lambda_init)
        x = _ffn_call(yT, part, gb, x, w_branch[l, 1].astype(BF16), w_o[l].astype(BF16),
                      row2(ln1_g[l]), row2(ln1_b[l]), w_gate_up[l].astype(BF16), w_down[l].astype(BF16),
                      row2(ln2_g[l]), row2(ln2_b[l]), alpha=alpha)
    return x
```

```python
import functools
import math

import jax
import jax.numpy as jnp
from jax import lax
from jax.experimental import pallas as pl
from jax.experimental.pallas import tpu as pltpu

F32 = jnp.float32
BF16 = jnp.bfloat16

CHUNK = 128
SGU_GROUPS = 4
DIFF_HEADS = 4
HEAD_DIM = 64
ROPE_DIM = HEAD_DIM // 4
ROPE_HALF = ROPE_DIM // 2
ROPE_THETA = 500000.0
CONV_K = 3
EPS = 1e-5
LOG2E = 1.4426950408889634
NEG_BIG = -1e30
SUM_ROWS = 16

PROJ_TILE = 1024
ATTN_Q_BLOCK = 512
ATTN_K_BLOCK = 256
FFN_TILE = 512
VMEM_LIMIT = 56 * 1024 * 1024


def _dot(a, b):
    return jnp.dot(a, b, preferred_element_type=F32)


def _layer_norm(x, g, b):
    mu = jnp.mean(x, axis=-1, keepdims=True)
    xc = x - mu
    var = jnp.mean(xc * xc, axis=-1, keepdims=True)
    return xc * lax.rsqrt(var + EPS) * g + b


def _resident(arr, *lead):
    tail = arr.shape[len(lead):]
    return pl.BlockSpec((None,) * len(lead) + tail, lambda *_: lead + (0,) * len(tail),
                        pipeline_mode=pl.Buffered(1))


def _in_proj_body(x_ref, cos_ref, sin_ref, w_in_ref, wsgu_ref, bsgu_ref, lng_ref, lnb_ref,
                  convw_ref, wb0_ref, wb2_ref,
                  qT_ref, k_ref, vT_ref, part_ref, gb_ref, carry_ref, *, d_model):
    tm = x_ref.shape[0]
    half = d_model // 2
    t = pl.program_id(1)
    xb = x_ref[...].astype(BF16)

    def proj(lo, hi):
        return _dot(xb, w_in_ref[:, lo:hi])

    o1 = d_model
    qk_w = 2 * DIFF_HEADS * HEAD_DIM
    v_w = DIFF_HEADS * 2 * HEAD_DIM
    o3 = o1 + 2 * qk_w
    o4 = o3 + v_w
    o5 = o4 + 3 * half

    za = proj(0, d_model)
    zq_t = proj(o1, o1 + qk_w).T
    zk_t = proj(o1 + qk_w, o3).T

    za = 0.5 * za * (1.0 + lax.erf(za * (1.0 / math.sqrt(2.0))))
    u = za[:, :half]
    v = _layer_norm(za[:, half:], lng_ref[...], lnb_ref[...]).astype(BF16)

    cos, sin = cos_ref[...], sin_ref[...]

    def rope_t(z_t):
        pieces = []
        for lo in range(0, qk_w, HEAD_DIM):
            r1, r2 = z_t[lo:lo + ROPE_HALF], z_t[lo + ROPE_HALF:lo + ROPE_DIM]
            pieces += [r1 * cos - r2 * sin, r2 * cos + r1 * sin, z_t[lo + ROPE_DIM:lo + HEAD_DIM]]
        return jnp.concatenate(pieces, axis=0)

    q_t = (rope_t(zq_t) * (HEAD_DIM ** -0.5 * LOG2E)).astype(BF16)
    no_q = jnp.zeros((HEAD_DIM, tm), BF16)
    maps = []
    for lo in range(0, qk_w, 2 * HEAD_DIM):
        maps += [q_t[lo:lo + HEAD_DIM], no_q, no_q, q_t[lo + HEAD_DIM:lo + 2 * HEAD_DIM]]
    qT_ref[...] = jnp.concatenate(maps, axis=0)
    k_ref[...] = rope_t(zk_t).T.astype(BF16)
    vT_ref[...] = proj(o3, o4).T.astype(BF16)
    gb_ref[...] = jax.nn.sigmoid(proj(o5 + d_model, o5 + 2 * d_model)).astype(BF16)

    gw = half // SGU_GROUPS
    rows = []
    for c in range(tm // CHUNK):
        cols = [_dot(wsgu_ref[g], v[c * CHUNK:(c + 1) * CHUNK, g * gw:(g + 1) * gw])
                for g in range(SGU_GROUPS)]
        rows.append(jnp.concatenate(cols, axis=1) + bsgu_ref[...])
    y_a = (u * jnp.concatenate(rows, axis=0)).astype(BF16)

    zc = proj(o4, o5)
    h = zc[:, half:2 * half] * zc[:, 2 * half:]

    @pl.when(t == 0)
    def _():
        carry_ref[...] = jnp.zeros_like(carry_ref)

    prev = carry_ref[...]
    row = lax.broadcasted_iota(jnp.int32, (tm, half), 0)
    h1 = jnp.where(row == 0, prev[7:8, :], pltpu.roll(h, 1, 0))
    h2 = jnp.where(row == 0, prev[6:7, :], jnp.where(row == 1, prev[7:8, :], pltpu.roll(h, 2, 0)))
    carry_ref[...] = h[tm - 8:, :]
    conv = h2 * convw_ref[0:1, :] + h1 * convw_ref[1:2, :] + h * convw_ref[2:3, :]
    y_c = (zc[:, :half] * conv).astype(BF16)

    g_a = jax.nn.sigmoid(proj(o5, o5 + d_model))
    g_c = jax.nn.sigmoid(proj(o5 + 2 * d_model, o5 + 3 * d_model))
    part_ref[...] = (g_a * _dot(y_a, wb0_ref[...]) + g_c * _dot(y_c, wb2_ref[...])).astype(BF16)


def _in_proj_call(x, rope_cos, rope_sin, layer, w_in, w_sgu, b_sgu, ln_g, ln_b, conv_w, w_branch):
    bsz, s_len, d_model = x.shape
    tm = min(PROJ_TILE, s_len)
    half = d_model // 2
    qk_w = 2 * DIFF_HEADS * HEAD_DIM
    v_w = DIFF_HEADS * 2 * HEAD_DIM
    grid = (bsz, s_len // tm)
    tok = lambda w: pl.BlockSpec((None, tm, w), lambda b, t: (b, t, 0))
    tokT = lambda w: pl.BlockSpec((None, w, tm), lambda b, t: (b, 0, t))
    return pl.pallas_call(
        functools.partial(_in_proj_body, d_model=d_model),
        grid=grid,
        in_specs=[tok(d_model), tokT(ROPE_HALF), tokT(ROPE_HALF),
                  _resident(w_in, layer), _resident(w_sgu, layer), _resident(b_sgu),
                  _resident(ln_g), _resident(ln_b), _resident(conv_w),
                  _resident(w_branch, layer, 0), _resident(w_branch, layer, 2)],
        out_specs=[tokT(2 * qk_w), tok(qk_w), tokT(v_w), tok(d_model), tok(d_model)],
        out_shape=[jax.ShapeDtypeStruct((bsz, 2 * qk_w, s_len), BF16),
                   jax.ShapeDtypeStruct((bsz, s_len, qk_w), BF16),
                   jax.ShapeDtypeStruct((bsz, v_w, s_len), BF16),
                   jax.ShapeDtypeStruct((bsz, s_len, d_model), BF16),
                   jax.ShapeDtypeStruct((bsz, s_len, d_model), BF16)],
        scratch_shapes=[pltpu.VMEM((8, half), F32)],
        compiler_params=pltpu.CompilerParams(
            dimension_semantics=("arbitrary", "arbitrary"), vmem_limit_bytes=VMEM_LIMIT),
        name="in_proj",
    )(x, rope_cos, rope_sin, w_in, w_sgu, b_sgu, ln_g, ln_b, conv_w, w_branch, w_branch)


def _attn_body(lam_ref, qm_ref, qn_ref, k_ref, vT_ref, subg_ref, o_ref, sa_ref, sb_ref, acc_ref, m_ref,
               mba_ref, mbb_ref, *, out_scale):
    n_maps, bk, bq = sa_ref.shape
    assert bq == 2 * bk and n_maps == 2 * DIFF_HEADS
    half = bq // 2
    i = pl.program_id(1)
    n_off = 2 * i
    lam = lam_ref[0, 0]
    pair_w = 2 * HEAD_DIM
    visible = (lax.broadcasted_iota(jnp.int32, (bk, bq), 0)
               <= lax.broadcasted_iota(jnp.int32, (bk, bq), 1))

    def colmax(s):
        return jnp.max(s, axis=0, keepdims=True)

    def rows(r):
        return slice(r * pair_w, (r + 1) * pair_w)

    def keys(kb):
        return pl.ds(pl.multiple_of(kb * bk, bk), bk)

    def values(h, kb):
        return jnp.concatenate([vT_ref[rows(h), keys(kb)], jnp.ones((SUM_ROWS, bk), BF16)], axis=0)

    def scores(idx, kb, q_lo=0, q_ref=qm_ref):
        return _dot(k_ref[keys(kb), rows(idx // 2)], q_ref[rows(idx), q_lo:])

    def stage(s_ref, mb_ref, idx, kb, q_ref=qm_ref):
        s = scores(idx, kb, 0, q_ref)
        s_ref[idx] = s
        mb_ref[idx] = colmax(s)

    def update(s, idx, kb, mb, q_lo=0):
        m_run = m_ref[idx, :, q_lo:]
        m_new = jnp.maximum(m_run, mb)
        p = jnp.exp2(s - m_new).astype(BF16)
        acc_ref[idx, :, q_lo:] = (jnp.exp2(m_run - m_new) * acc_ref[idx, :, q_lo:]
                                  + _dot(values(idx // 2, kb), p))
        m_ref[idx, :, q_lo:] = m_new

    @pl.when(i == 0)
    def _():
        for idx in range(n_maps):
            stage(sa_ref, mba_ref, idx, 0)

    acc_ref[...] = jnp.zeros_like(acc_ref)
    m_ref[...] = jnp.full_like(m_ref, NEG_BIG)

    def pair(j):
        for idx in range(n_maps):
            stage(sb_ref, mbb_ref, idx, j + 1)
            update(sa_ref[idx], idx, j, mba_ref[idx])
        for idx in range(n_maps):
            stage(sa_ref, mba_ref, idx, j + 2)
            update(sb_ref[idx], idx, j + 1, mbb_ref[idx])

    odd = i % 2

    @pl.when(odd == 1)
    def _():
        pair(0)

    @pl.loop(0, i // 2)
    def _(t):
        j = 2 * odd + 4 * t
        pair(j)
        pair(j + 2)

    for idx in range(n_maps):
        sb_ref[idx, :, half:] = jnp.where(visible[:, :half], scores(idx, n_off + 1, half), NEG_BIG)
        s = jnp.where(visible, sa_ref[idx], NEG_BIG)
        update(s, idx, n_off, colmax(s))
        stage(sa_ref, mba_ref, idx, 0, qn_ref)
    for h in range(DIFF_HEADS):
        o_maps = []
        for idx in (2 * h, 2 * h + 1):
            s = sb_ref[idx, :, half:]
            update(s, idx, n_off + 1, colmax(s), half)
            o_maps.append(acc_ref[idx, :pair_w, :] * (1.0 / acc_ref[idx, pair_w:pair_w + 1, :]))
        o = o_maps[0] - lam * o_maps[1]
        ms = jnp.mean(o * o, axis=0, keepdims=True)
        o_ref[rows(h), :] = o * lax.rsqrt(ms + EPS) * subg_ref[...] * out_scale


def _attn_call(lam, qm, k, vT, subg, *, out_scale):
    bsz, s_len, _ = k.shape
    bq, bk = ATTN_Q_BLOCK, ATTN_K_BLOCK
    assert bq == 2 * bk and s_len % bq == 0
    n_maps = 2 * DIFF_HEADS
    n_q = s_len // bq
    grid = (bsz, n_q)
    per_batch = lambda shape: pl.BlockSpec((None,) + shape, lambda b, i: (b, 0, 0))
    return pl.pallas_call(
        functools.partial(_attn_body, out_scale=out_scale),
        grid=grid,
        in_specs=[pl.BlockSpec(memory_space=pltpu.SMEM),
                  pl.BlockSpec((None, qm.shape[1], bq), lambda b, i: (b, 0, i)),
                  pl.BlockSpec((None, qm.shape[1], bq), lambda b, i: (b, 0, jnp.minimum(i + 1, n_q - 1))),
                  per_batch(k.shape[1:]), per_batch(vT.shape[1:]), _resident(subg)],
        out_specs=pl.BlockSpec((None, vT.shape[1], bq), lambda b, i: (b, 0, i)),
        out_shape=jax.ShapeDtypeStruct((bsz, vT.shape[1], s_len), F32),
        scratch_shapes=[pltpu.VMEM((n_maps, bk, bq), F32), pltpu.VMEM((n_maps, bk, bq), F32),
                        pltpu.VMEM((n_maps, 2 * HEAD_DIM + SUM_ROWS, bq), F32),
                        pltpu.VMEM((n_maps, 1, bq), F32), pltpu.VMEM((n_maps, 1, bq), F32),
                        pltpu.VMEM((n_maps, 1, bq), F32)],
        compiler_params=pltpu.CompilerParams(
            dimension_semantics=("arbitrary", "arbitrary"), vmem_limit_bytes=VMEM_LIMIT),
        name="diff_attn",
    )(lam, qm, qm, k, vT, subg)


def _ffn_chunks(d_ff):
    step = 1024
    return [(lo, min(lo + step, d_ff)) for lo in range(0, d_ff, step)]


def _ffn_body(yT_ref, part_ref, gb_ref, x_ref, wb1_ref, wo_ref, g1_ref, b1_ref, wgu_ref, wd_ref,
              g2_ref, b2_ref, o_ref, *, alpha, d_ff):
    tm = x_ref.shape[0]
    halves = [slice(lo, lo + tm // 2) for lo in (0, tm // 2)]
    y_b = yT_ref[...].T.astype(BF16)

    branch = [_dot(y_b[rs, :], wb1_ref[...]) for rs in halves]
    x1 = []
    for rs, br in zip(halves, branch):
        merged = part_ref[rs, :].astype(F32) + gb_ref[rs, :].astype(F32) * br
        mix = _dot(merged.astype(BF16), wo_ref[...])
        x1.append(_layer_norm(alpha * x_ref[rs, :] + mix, g1_ref[...], b1_ref[...]))
    for rs, x in zip(halves, x1):
        xb = x.astype(BF16)
        acc = alpha * x
        for lo, hi in _ffn_chunks(d_ff):
            gate = _dot(xb, wgu_ref[:, lo:hi])
            up = _dot(xb, wgu_ref[:, d_ff + lo:d_ff + hi])
            hidden = (gate * jax.nn.sigmoid(gate) * up).astype(BF16)
            acc = acc + _dot(hidden, wd_ref[lo:hi, :])
        o_ref[rs, :] = _layer_norm(acc, g2_ref[...], b2_ref[...])


def _ffn_call(yT, part, gb, x, layer, w_branch, w_o, g1, b1, w_gate_up, w_down, g2, b2, *, alpha):
    bsz, s_len, d_model = x.shape
    d_ff = w_down.shape[1]
    tm = min(FFN_TILE, s_len)
    tok = lambda w: pl.BlockSpec((None, tm, w), lambda b, t: (b, t, 0))
    weights = (w_branch, w_o, g1, b1, w_gate_up, w_down, g2, b2)
    weight_specs = [_resident(w_branch, layer, 1), _resident(w_o, layer), _resident(g1), _resident(b1),
                    _resident(w_gate_up, layer), _resident(w_down, layer), _resident(g2), _resident(b2)]
    return pl.pallas_call(
        functools.partial(_ffn_body, alpha=alpha, d_ff=d_ff),
        grid=(bsz, s_len // tm),
        in_specs=[pl.BlockSpec((None, yT.shape[1], tm), lambda b, t: (b, 0, t)),
                  tok(d_model), tok(d_model), tok(d_model)] + weight_specs,
        out_specs=tok(d_model),
        out_shape=jax.ShapeDtypeStruct((bsz, s_len, d_model), F32),
        compiler_params=pltpu.CompilerParams(
            dimension_semantics=("arbitrary", "arbitrary"), vmem_limit_bytes=VMEM_LIMIT),
        name="merge_swiglu",
    )(yT, part, gb, x, *weights)


def _rope_tables(positions):
    inv_freq = ROPE_THETA ** (-jnp.arange(0, ROPE_DIM, 2, dtype=F32) / ROPE_DIM)
    ang = positions.astype(F32)[:, None, :] * inv_freq[None, :, None]
    return jnp.cos(ang), jnp.sin(ang)


def kernel(x, positions, w_in, w_sgu, b_sgu, sgu_ln_g, sgu_ln_b, lambda_q1, lambda_k1, lambda_q2,
           lambda_k2, subln_g, conv_w, w_branch, w_o, ln1_g, ln1_b, w_gate_up, w_down, ln2_g, ln2_b):
    bsz, s_len, d_model = x.shape
    depth = w_in.shape[0]
    half = d_model // 2
    alpha = (2 * depth) ** 0.25
    rope_cos, rope_sin = _rope_tables(positions)
    tril = jnp.tril(jnp.ones((CHUNK, CHUNK), dtype=bool))
    row2 = lambda a: a.reshape(1, -1)
    w_in, w_branch, w_o, w_gate_up, w_down = (w.astype(BF16) for w in (w_in, w_branch, w_o, w_gate_up, w_down))
    w_sgu = jnp.where(tril, w_sgu, 0.0).astype(BF16)

    for l in range(depth):
        lambda_init = 0.8 - 0.6 * math.exp(-0.3 * l)
        lam = (jnp.exp(jnp.sum(lambda_q1[l] * lambda_k1[l])) - jnp.exp(jnp.sum(lambda_q2[l] * lambda_k2[l]))
               + lambda_init).reshape(1, 1).astype(F32)
        b_s = jnp.repeat(b_sgu[l].T, half // SGU_GROUPS, axis=1)
        qT, k, vT, part, gb = _in_proj_call(
            x, rope_cos, rope_sin, l, w_in, w_sgu, b_s, row2(sgu_ln_g[l]), row2(sgu_ln_b[l]),
            conv_w[l].T, w_branch)
        yT = _attn_call(lam, qT, k, vT, subln_g[l].reshape(-1, 1), out_scale=1.0 - lambda_init)
        x = _ffn_call(yT, part, gb, x, l, w_branch, w_o, row2(ln1_g[l]), row2(ln1_b[l]),
                      w_gate_up, w_down, row2(ln2_g[l]), row2(ln2_b[l]), alpha=alpha)
    return x
```

```python
import functools
import math

import jax
import jax.numpy as jnp
from jax import lax
from jax.experimental import pallas as pl
from jax.experimental.pallas import tpu as pltpu

F32 = jnp.float32
BF16 = jnp.bfloat16

CHUNK = 128
SGU_GROUPS = 4
DIFF_HEADS = 4
HEAD_DIM = 64
ROPE_DIM = HEAD_DIM // 4
ROPE_HALF = ROPE_DIM // 2
ROPE_THETA = 500000.0
CONV_K = 3
EPS = 1e-5
LOG2E = 1.4426950408889634
NEG_BIG = -1e30
SUM_ROWS = 16

PROJ_TILE = 1024
ATTN_Q_BLOCK = 512
ATTN_K_BLOCK = 256
FFN_TILE = 1024
FFN_ROWS = 256
VMEM_LIMIT = 56 * 1024 * 1024


def _dot(a, b):
    return jnp.dot(a, b, preferred_element_type=F32)


def _layer_norm(x, g, b):
    mu = jnp.mean(x, axis=-1, keepdims=True)
    xc = x - mu
    var = jnp.mean(xc * xc, axis=-1, keepdims=True)
    return xc * lax.rsqrt(var + EPS) * g + b


def _resident(arr, *lead):
    tail = arr.shape[len(lead):]
    return pl.BlockSpec((None,) * len(lead) + tail, lambda *_: lead + (0,) * len(tail),
                        pipeline_mode=pl.Buffered(1))


def _in_proj_body(x_ref, cos_ref, sin_ref, w_in_ref, wsgu_ref, bsgu_ref, lng_ref, lnb_ref,
                  convw_ref, wb0_ref, wb2_ref,
                  qT_ref, k_ref, vT_ref, part_ref, gb_ref, carry_ref, *, d_model):
    tm = x_ref.shape[0]
    half = d_model // 2
    t = pl.program_id(1)
    xb = x_ref[...].astype(BF16)

    def proj(lo, hi):
        return _dot(xb, w_in_ref[:, lo:hi])

    o1 = d_model
    qk_w = 2 * DIFF_HEADS * HEAD_DIM
    v_w = DIFF_HEADS * 2 * HEAD_DIM
    o3 = o1 + 2 * qk_w
    o4 = o3 + v_w
    o5 = o4 + 3 * half

    za = proj(0, d_model)
    zq_t = proj(o1, o1 + qk_w).T
    zk_t = proj(o1 + qk_w, o3).T

    za = 0.5 * za * (1.0 + lax.erf(za * (1.0 / math.sqrt(2.0))))
    u = za[:, :half]
    v = _layer_norm(za[:, half:], lng_ref[...], lnb_ref[...]).astype(BF16)

    cos, sin = cos_ref[...], sin_ref[...]

    def rope_t(z_t):
        pieces = []
        for lo in range(0, qk_w, HEAD_DIM):
            r1, r2 = z_t[lo:lo + ROPE_HALF], z_t[lo + ROPE_HALF:lo + ROPE_DIM]
            pieces += [r1 * cos - r2 * sin, r2 * cos + r1 * sin, z_t[lo + ROPE_DIM:lo + HEAD_DIM]]
        return jnp.concatenate(pieces, axis=0)

    q_t = (rope_t(zq_t) * (HEAD_DIM ** -0.5 * LOG2E)).astype(BF16)
    no_q = jnp.zeros((HEAD_DIM, tm), BF16)
    maps = []
    for lo in range(0, qk_w, 2 * HEAD_DIM):
        maps += [q_t[lo:lo + HEAD_DIM], no_q, no_q, q_t[lo + HEAD_DIM:lo + 2 * HEAD_DIM]]
    qT_ref[...] = jnp.concatenate(maps, axis=0)
    k_ref[...] = rope_t(zk_t).T.astype(BF16)
    vT_ref[...] = proj(o3, o4).T.astype(BF16)
    gb_ref[...] = jax.nn.sigmoid(proj(o5 + d_model, o5 + 2 * d_model)).astype(BF16)

    gw = half // SGU_GROUPS
    rows = []
    for c in range(tm // CHUNK):
        cols = [_dot(wsgu_ref[g], v[c * CHUNK:(c + 1) * CHUNK, g * gw:(g + 1) * gw])
                for g in range(SGU_GROUPS)]
        rows.append(jnp.concatenate(cols, axis=1) + bsgu_ref[...])
    y_a = (u * jnp.concatenate(rows, axis=0)).astype(BF16)

    zc = proj(o4, o5)
    h = zc[:, half:2 * half] * zc[:, 2 * half:]

    @pl.when(t == 0)
    def _():
        carry_ref[...] = jnp.zeros_like(carry_ref)

    prev = carry_ref[...]
    row = lax.broadcasted_iota(jnp.int32, (tm, half), 0)
    h1 = jnp.where(row == 0, prev[7:8, :], pltpu.roll(h, 1, 0))
    h2 = jnp.where(row == 0, prev[6:7, :], jnp.where(row == 1, prev[7:8, :], pltpu.roll(h, 2, 0)))
    carry_ref[...] = h[tm - 8:, :]
    conv = h2 * convw_ref[0:1, :] + h1 * convw_ref[1:2, :] + h * convw_ref[2:3, :]
    y_c = (zc[:, :half] * conv).astype(BF16)

    g_a = jax.nn.sigmoid(proj(o5, o5 + d_model))
    g_c = jax.nn.sigmoid(proj(o5 + 2 * d_model, o5 + 3 * d_model))
    part_ref[...] = (g_a * _dot(y_a, wb0_ref[...]) + g_c * _dot(y_c, wb2_ref[...])).astype(BF16)


def _in_proj_call(x, rope_cos, rope_sin, layer, w_in, w_sgu, b_sgu, ln_g, ln_b, conv_w, w_branch):
    bsz, s_len, d_model = x.shape
    tm = min(PROJ_TILE, s_len)
    half = d_model // 2
    qk_w = 2 * DIFF_HEADS * HEAD_DIM
    v_w = DIFF_HEADS * 2 * HEAD_DIM
    grid = (bsz, s_len // tm)
    tok = lambda w: pl.BlockSpec((None, tm, w), lambda b, t: (b, t, 0))
    tokT = lambda w: pl.BlockSpec((None, w, tm), lambda b, t: (b, 0, t))
    return pl.pallas_call(
        functools.partial(_in_proj_body, d_model=d_model),
        grid=grid,
        in_specs=[tok(d_model), tokT(ROPE_HALF), tokT(ROPE_HALF),
                  _resident(w_in, layer), _resident(w_sgu, layer), _resident(b_sgu),
                  _resident(ln_g), _resident(ln_b), _resident(conv_w),
                  _resident(w_branch, layer, 0), _resident(w_branch, layer, 2)],
        out_specs=[tokT(2 * qk_w), tok(qk_w), tokT(v_w), tok(d_model), tok(d_model)],
        out_shape=[jax.ShapeDtypeStruct((bsz, 2 * qk_w, s_len), BF16),
                   jax.ShapeDtypeStruct((bsz, s_len, qk_w), BF16),
                   jax.ShapeDtypeStruct((bsz, v_w, s_len), BF16),
                   jax.ShapeDtypeStruct((bsz, s_len, d_model), BF16),
                   jax.ShapeDtypeStruct((bsz, s_len, d_model), BF16)],
        scratch_shapes=[pltpu.VMEM((8, half), F32)],
        compiler_params=pltpu.CompilerParams(
            dimension_semantics=("arbitrary", "arbitrary"), vmem_limit_bytes=VMEM_LIMIT),
        name="in_proj",
    )(x, rope_cos, rope_sin, w_in, w_sgu, b_sgu, ln_g, ln_b, conv_w, w_branch, w_branch)


def _attn_body(lam_ref, qm_ref, qn_ref, k_ref, vT_ref, subg_ref, o_ref, sa_ref, sb_ref, acc_ref, m_ref,
               mba_ref, mbb_ref, *, out_scale):
    n_maps, bk, bq = sa_ref.shape
    assert bq == 2 * bk and n_maps == 2 * DIFF_HEADS
    half = bq // 2
    i = pl.program_id(1)
    n_off = 2 * i
    lam = lam_ref[0, 0]
    pair_w = 2 * HEAD_DIM
    visible = (lax.broadcasted_iota(jnp.int32, (bk, bq), 0)
               <= lax.broadcasted_iota(jnp.int32, (bk, bq), 1))

    def colmax(s):
        return jnp.max(s, axis=0, keepdims=True)

    def rows(r):
        return slice(r * pair_w, (r + 1) * pair_w)

    def keys(kb):
        return pl.ds(pl.multiple_of(kb * bk, bk), bk)

    def values(h, kb):
        return jnp.concatenate([vT_ref[rows(h), keys(kb)], jnp.ones((SUM_ROWS, bk), BF16)], axis=0)

    def scores(idx, kb, q_lo=0, q_ref=qm_ref):
        return _dot(k_ref[keys(kb), rows(idx // 2)], q_ref[rows(idx), q_lo:])

    def stage(s_ref, mb_ref, idx, kb, q_ref=qm_ref):
        s = scores(idx, kb, 0, q_ref)
        s_ref[idx] = s
        mb_ref[idx] = colmax(s)

    def update(s, idx, kb, mb, q_lo=0):
        m_run = m_ref[idx, :, q_lo:]
        m_new = jnp.maximum(m_run, mb)
        p = jnp.exp2(s - m_new).astype(BF16)
        acc_ref[idx, :, q_lo:] = (jnp.exp2(m_run - m_new) * acc_ref[idx, :, q_lo:]
                                  + _dot(values(idx // 2, kb), p))
        m_ref[idx, :, q_lo:] = m_new

    @pl.when(i == 0)
    def _():
        for idx in range(n_maps):
            stage(sa_ref, mba_ref, idx, 0)

    acc_ref[...] = jnp.zeros_like(acc_ref)
    m_ref[...] = jnp.full_like(m_ref, NEG_BIG)

    def pair(j):
        for idx in range(n_maps):
            stage(sb_ref, mbb_ref, idx, j + 1)
            update(sa_ref[idx], idx, j, mba_ref[idx])
        for idx in range(n_maps):
            stage(sa_ref, mba_ref, idx, j + 2)
            update(sb_ref[idx], idx, j + 1, mbb_ref[idx])

    odd = i % 2

    @pl.when(odd == 1)
    def _():
        pair(0)

    @pl.loop(0, i // 2)
    def _(t):
        j = 2 * odd + 4 * t
        pair(j)
        pair(j + 2)

    for idx in range(n_maps):
        sb_ref[idx, :, half:] = jnp.where(visible[:, :half], scores(idx, n_off + 1, half), NEG_BIG)
        s = jnp.where(visible, sa_ref[idx], NEG_BIG)
        update(s, idx, n_off, colmax(s))
        stage(sa_ref, mba_ref, idx, 0, qn_ref)
    for h in range(DIFF_HEADS):
        o_maps = []
        for idx in (2 * h, 2 * h + 1):
            s = sb_ref[idx, :, half:]
            update(s, idx, n_off + 1, colmax(s), half)
            o_maps.append(acc_ref[idx, :pair_w, :] * (1.0 / acc_ref[idx, pair_w:pair_w + 1, :]))
        o = o_maps[0] - lam * o_maps[1]
        ms = jnp.mean(o * o, axis=0, keepdims=True)
        o_ref[rows(h), :] = o * lax.rsqrt(ms + EPS) * subg_ref[...] * out_scale


def _attn_call(lam, qm, k, vT, subg, *, out_scale):
    bsz, s_len, _ = k.shape
    bq, bk = ATTN_Q_BLOCK, ATTN_K_BLOCK
    assert bq == 2 * bk and s_len % bq == 0
    n_maps = 2 * DIFF_HEADS
    n_q = s_len // bq
    grid = (bsz, n_q)
    per_batch = lambda shape: pl.BlockSpec((None,) + shape, lambda b, i: (b, 0, 0))
    return pl.pallas_call(
        functools.partial(_attn_body, out_scale=out_scale),
        grid=grid,
        in_specs=[pl.BlockSpec(memory_space=pltpu.SMEM),
                  pl.BlockSpec((None, qm.shape[1], bq), lambda b, i: (b, 0, i)),
                  pl.BlockSpec((None, qm.shape[1], bq), lambda b, i: (b, 0, jnp.minimum(i + 1, n_q - 1))),
                  per_batch(k.shape[1:]), per_batch(vT.shape[1:]), _resident(subg)],
        out_specs=pl.BlockSpec((None, vT.shape[1], bq), lambda b, i: (b, 0, i)),
        out_shape=jax.ShapeDtypeStruct((bsz, vT.shape[1], s_len), F32),
        scratch_shapes=[pltpu.VMEM((n_maps, bk, bq), F32), pltpu.VMEM((n_maps, bk, bq), F32),
                        pltpu.VMEM((n_maps, 2 * HEAD_DIM + SUM_ROWS, bq), F32),
                        pltpu.VMEM((n_maps, 1, bq), F32), pltpu.VMEM((n_maps, 1, bq), F32),
                        pltpu.VMEM((n_maps, 1, bq), F32)],
        compiler_params=pltpu.CompilerParams(
            dimension_semantics=("arbitrary", "arbitrary"), vmem_limit_bytes=VMEM_LIMIT),
        name="diff_attn",
    )(lam, qm, qm, k, vT, subg)


def _ffn_chunks(d_ff):
    step = 1024
    return [(lo, min(lo + step, d_ff)) for lo in range(0, d_ff, step)]


def _ffn_body(yT_ref, part_ref, gb_ref, x_ref, wb1_ref, wo_ref, g1_ref, b1_ref, wgu_ref, wd_ref,
              g2_ref, b2_ref, o_ref, *, alpha, d_ff):
    tm = x_ref.shape[0]
    halves = [slice(lo, lo + FFN_ROWS) for lo in range(0, tm, FFN_ROWS)]
    y_b = yT_ref[...].T.astype(BF16)

    branch = [_dot(y_b[rs, :], wb1_ref[...]) for rs in halves]
    x1 = []
    for rs, br in zip(halves, branch):
        merged = part_ref[rs, :].astype(F32) + gb_ref[rs, :].astype(F32) * br
        mix = _dot(merged.astype(BF16), wo_ref[...])
        x1.append(_layer_norm(alpha * x_ref[rs, :] + mix, g1_ref[...], b1_ref[...]))
    for rs, x in zip(halves, x1):
        xb = x.astype(BF16)
        acc = alpha * x
        for lo, hi in _ffn_chunks(d_ff):
            gate = _dot(xb, wgu_ref[:, lo:hi])
            up = _dot(xb, wgu_ref[:, d_ff + lo:d_ff + hi])
            hidden = (gate * jax.nn.sigmoid(gate) * up).astype(BF16)
            acc = acc + _dot(hidden, wd_ref[lo:hi, :])
        o_ref[rs, :] = _layer_norm(acc, g2_ref[...], b2_ref[...])


def _ffn_call(yT, part, gb, x, layer, w_branch, w_o, g1, b1, w_gate_up, w_down, g2, b2, *, alpha):
    bsz, s_len, d_model = x.shape
    d_ff = w_down.shape[1]
    tm = min(FFN_TILE, s_len)
    tok = lambda w: pl.BlockSpec((None, tm, w), lambda b, t: (b, t, 0))
    weights = (w_branch, w_o, g1, b1, w_gate_up, w_down, g2, b2)
    weight_specs = [_resident(w_branch, layer, 1), _resident(w_o, layer), _resident(g1), _resident(b1),
                    _resident(w_gate_up, layer), _resident(w_down, layer), _resident(g2), _resident(b2)]
    return pl.pallas_call(
        functools.partial(_ffn_body, alpha=alpha, d_ff=d_ff),
        grid=(bsz, s_len // tm),
        in_specs=[pl.BlockSpec((None, yT.shape[1], tm), lambda b, t: (b, 0, t)),
                  tok(d_model), tok(d_model), tok(d_model)] + weight_specs,
        out_specs=tok(d_model),
        out_shape=jax.ShapeDtypeStruct((bsz, s_len, d_model), F32),
        compiler_params=pltpu.CompilerParams(
            dimension_semantics=("arbitrary", "arbitrary"), vmem_limit_bytes=VMEM_LIMIT),
        name="merge_swiglu",
    )(yT, part, gb, x, *weights)


def _rope_tables(positions):
    inv_freq = ROPE_THETA ** (-jnp.arange(0, ROPE_DIM, 2, dtype=F32) / ROPE_DIM)
    ang = positions.astype(F32)[:, None, :] * inv_freq[None, :, None]
    return jnp.cos(ang), jnp.sin(ang)


def kernel(x, positions, w_in, w_sgu, b_sgu, sgu_ln_g, sgu_ln_b, lambda_q1, lambda_k1, lambda_q2,
           lambda_k2, subln_g, conv_w, w_branch, w_o, ln1_g, ln1_b, w_gate_up, w_down, ln2_g, ln2_b):
    bsz, s_len, d_model = x.shape
    depth = w_in.shape[0]
    half = d_model // 2
    alpha = (2 * depth) ** 0.25
    rope_cos, rope_sin = _rope_tables(positions)
    tril = jnp.tril(jnp.ones((CHUNK, CHUNK), dtype=bool))
    row2 = lambda a: a.reshape(1, -1)
    w_in, w_branch, w_o, w_gate_up, w_down = (w.astype(BF16) for w in (w_in, w_branch, w_o, w_gate_up, w_down))
    w_sgu = jnp.where(tril, w_sgu, 0.0).astype(BF16)

    for l in range(depth):
        lambda_init = 0.8 - 0.6 * math.exp(-0.3 * l)
        lam = (jnp.exp(jnp.sum(lambda_q1[l] * lambda_k1[l])) - jnp.exp(jnp.sum(lambda_q2[l] * lambda_k2[l]))
               + lambda_init).reshape(1, 1).astype(F32)
        b_s = jnp.repeat(b_sgu[l].T, half // SGU_GROUPS, axis=1)
        qT, k, vT, part, gb = _in_proj_call(
            x, rope_cos, rope_sin, l, w_in, w_sgu, b_s, row2(sgu_ln_g[l]), row2(sgu_ln_b[l]),
            conv_w[l].T, w_branch)
        yT = _attn_call(lam, qT, k, vT, subln_g[l].reshape(-1, 1), out_scale=1.0 - lambda_init)
        x = _ffn_call(yT, part, gb, x, l, w_branch, w_o, row2(ln1_g[l]), row2(ln1_b[l]),
                      w_gate_up, w_down, row2(ln2_g[l]), row2(ln2_b[l]), alpha=alpha)
    return x
```

```python
import functools
import math

import jax
import jax.numpy as jnp
from jax import lax
from jax.experimental import pallas as pl
from jax.experimental.pallas import tpu as pltpu

F32 = jnp.float32
BF16 = jnp.bfloat16

CHUNK = 128
SGU_GROUPS = 4
DIFF_HEADS = 4
HEAD_DIM = 64
ROPE_DIM = HEAD_DIM // 4
ROPE_HALF = ROPE_DIM // 2
ROPE_THETA = 500000.0
CONV_K = 3
EPS = 1e-5
LOG2E = 1.4426950408889634
NEG_BIG = -1e30
SUM_ROWS = 16

PROJ_TILE = 1024
PROJ_ROWS = 256
ATTN_Q_BLOCK = 512
ATTN_K_BLOCK = 256
FFN_TILE = 1024
FFN_ROWS = 256
VMEM_LIMIT = 56 * 1024 * 1024


def _dot(a, b):
    return jnp.dot(a, b, preferred_element_type=F32)


def _layer_norm(x, g, b):
    mu = jnp.mean(x, axis=-1, keepdims=True)
    xc = x - mu
    var = jnp.mean(xc * xc, axis=-1, keepdims=True)
    return xc * lax.rsqrt(var + EPS) * g + b


def _resident(arr, *lead):
    tail = arr.shape[len(lead):]
    return pl.BlockSpec((None,) * len(lead) + tail, lambda *_: lead + (0,) * len(tail),
                        pipeline_mode=pl.Buffered(1))


def _in_proj_body(x_ref, cos_ref, sin_ref, w_in_ref, wsgu_ref, bsgu_ref, lng_ref, lnb_ref,
                  convw_ref, wb0_ref, wb2_ref,
                  qT_ref, k_ref, vT_ref, part_ref, gb_ref, carry_ref, *, d_model):
    tm = x_ref.shape[0]
    half = d_model // 2
    o1 = d_model
    qk_w = 2 * DIFF_HEADS * HEAD_DIM
    v_w = DIFF_HEADS * 2 * HEAD_DIM
    o3 = o1 + 2 * qk_w
    o4 = o3 + v_w
    o5 = o4 + 3 * half

    def piece(rs, prev):
        rows = rs.stop - rs.start
        xb = x_ref[rs, :].astype(BF16)

        def proj(lo, hi):
            return _dot(xb, w_in_ref[:, lo:hi])

        za = proj(0, d_model)
        zq_t = proj(o1, o1 + qk_w).T
        zk_t = proj(o1 + qk_w, o3).T

        za = 0.5 * za * (1.0 + lax.erf(za * (1.0 / math.sqrt(2.0))))
        u = za[:, :half]
        v = _layer_norm(za[:, half:], lng_ref[...], lnb_ref[...]).astype(BF16)

        cos, sin = cos_ref[:, rs], sin_ref[:, rs]

        def rope_t(z_t):
            pieces = []
            for lo in range(0, qk_w, HEAD_DIM):
                r1, r2 = z_t[lo:lo + ROPE_HALF], z_t[lo + ROPE_HALF:lo + ROPE_DIM]
                pieces += [r1 * cos - r2 * sin, r2 * cos + r1 * sin, z_t[lo + ROPE_DIM:lo + HEAD_DIM]]
            return jnp.concatenate(pieces, axis=0)

        q_t = (rope_t(zq_t) * (HEAD_DIM ** -0.5 * LOG2E)).astype(BF16)
        no_q = jnp.zeros((HEAD_DIM, rows), BF16)
        maps = []
        for lo in range(0, qk_w, 2 * HEAD_DIM):
            maps += [q_t[lo:lo + HEAD_DIM], no_q, no_q, q_t[lo + HEAD_DIM:lo + 2 * HEAD_DIM]]
        qT_ref[:, rs] = jnp.concatenate(maps, axis=0)
        k_ref[rs, :] = rope_t(zk_t).T.astype(BF16)
        vT_ref[:, rs] = proj(o3, o4).T.astype(BF16)
        gb_ref[rs, :] = jax.nn.sigmoid(proj(o5 + d_model, o5 + 2 * d_model)).astype(BF16)

        gw = half // SGU_GROUPS
        mixed = []
        for c in range(rows // CHUNK):
            cols = [_dot(wsgu_ref[g], v[c * CHUNK:(c + 1) * CHUNK, g * gw:(g + 1) * gw])
                    for g in range(SGU_GROUPS)]
            mixed.append(jnp.concatenate(cols, axis=1) + bsgu_ref[...])
        y_a = (u * jnp.concatenate(mixed, axis=0)).astype(BF16)

        zc = proj(o4, o5)
        h = zc[:, half:2 * half] * zc[:, 2 * half:]
        row = lax.broadcasted_iota(jnp.int32, (rows, half), 0)
        h1 = jnp.where(row == 0, prev[7:8, :], pltpu.roll(h, 1, 0))
        h2 = jnp.where(row == 0, prev[6:7, :], jnp.where(row == 1, prev[7:8, :], pltpu.roll(h, 2, 0)))
        conv = h2 * convw_ref[0:1, :] + h1 * convw_ref[1:2, :] + h * convw_ref[2:3, :]
        y_c = (zc[:, :half] * conv).astype(BF16)

        g_a = jax.nn.sigmoid(proj(o5, o5 + d_model))
        g_c = jax.nn.sigmoid(proj(o5 + 2 * d_model, o5 + 3 * d_model))
        part_ref[rs, :] = (g_a * _dot(y_a, wb0_ref[...]) + g_c * _dot(y_c, wb2_ref[...])).astype(BF16)
        return h[rows - 8:, :]

    @pl.when(pl.program_id(1) == 0)
    def _():
        carry_ref[...] = jnp.zeros_like(carry_ref)

    prev = carry_ref[...]
    for lo in range(0, tm, PROJ_ROWS):
        prev = piece(slice(lo, lo + PROJ_ROWS), prev)
    carry_ref[...] = prev


def _in_proj_call(x, rope_cos, rope_sin, layer, w_in, w_sgu, b_sgu, ln_g, ln_b, conv_w, w_branch):
    bsz, s_len, d_model = x.shape
    tm = min(PROJ_TILE, s_len)
    half = d_model // 2
    qk_w = 2 * DIFF_HEADS * HEAD_DIM
    v_w = DIFF_HEADS * 2 * HEAD_DIM
    grid = (bsz, s_len // tm)
    tok = lambda w: pl.BlockSpec((None, tm, w), lambda b, t: (b, t, 0))
    tokT = lambda w: pl.BlockSpec((None, w, tm), lambda b, t: (b, 0, t))
    return pl.pallas_call(
        functools.partial(_in_proj_body, d_model=d_model),
        grid=grid,
        in_specs=[tok(d_model), tokT(ROPE_HALF), tokT(ROPE_HALF),
                  _resident(w_in, layer), _resident(w_sgu, layer), _resident(b_sgu),
                  _resident(ln_g), _resident(ln_b), _resident(conv_w),
                  _resident(w_branch, layer, 0), _resident(w_branch, layer, 2)],
        out_specs=[tokT(2 * qk_w), tok(qk_w), tokT(v_w), tok(d_model), tok(d_model)],
        out_shape=[jax.ShapeDtypeStruct((bsz, 2 * qk_w, s_len), BF16),
                   jax.ShapeDtypeStruct((bsz, s_len, qk_w), BF16),
                   jax.ShapeDtypeStruct((bsz, v_w, s_len), BF16),
                   jax.ShapeDtypeStruct((bsz, s_len, d_model), BF16),
                   jax.ShapeDtypeStruct((bsz, s_len, d_model), BF16)],
        scratch_shapes=[pltpu.VMEM((8, half), F32)],
        compiler_params=pltpu.CompilerParams(
            dimension_semantics=("arbitrary", "arbitrary"), vmem_limit_bytes=VMEM_LIMIT),
        name="in_proj",
    )(x, rope_cos, rope_sin, w_in, w_sgu, b_sgu, ln_g, ln_b, conv_w, w_branch, w_branch)


def _attn_body(lam_ref, qm_ref, qn_ref, k_ref, vT_ref, subg_ref, o_ref, sa_ref, sb_ref, acc_ref, m_ref,
               mba_ref, mbb_ref, *, out_scale):
    n_maps, bk, bq = sa_ref.shape
    assert bq == 2 * bk and n_maps == 2 * DIFF_HEADS
    half = bq // 2
    i = pl.program_id(1)
    n_off = 2 * i
    lam = lam_ref[0, 0]
    pair_w = 2 * HEAD_DIM
    visible = (lax.broadcasted_iota(jnp.int32, (bk, bq), 0)
               <= lax.broadcasted_iota(jnp.int32, (bk, bq), 1))

    def colmax(s):
        return jnp.max(s, axis=0, keepdims=True)

    def rows(r):
        return slice(r * pair_w, (r + 1) * pair_w)

    def keys(kb):
        return pl.ds(pl.multiple_of(kb * bk, bk), bk)

    def values(h, kb):
        return jnp.concatenate([vT_ref[rows(h), keys(kb)], jnp.ones((SUM_ROWS, bk), BF16)], axis=0)

    def scores(idx, kb, q_lo=0, q_ref=qm_ref):
        return _dot(k_ref[keys(kb), rows(idx // 2)], q_ref[rows(idx), q_lo:])

    def stage(s_ref, mb_ref, idx, kb, q_ref=qm_ref):
        s = scores(idx, kb, 0, q_ref)
        s_ref[idx] = s
        mb_ref[idx] = colmax(s)

    def update(s, idx, kb, mb, q_lo=0):
        m_run = m_ref[idx, :, q_lo:]
        m_new = jnp.maximum(m_run, mb)
        p = jnp.exp2(s - m_new).astype(BF16)
        acc_ref[idx, :, q_lo:] = (jnp.exp2(m_run - m_new) * acc_ref[idx, :, q_lo:]
                                  + _dot(values(idx // 2, kb), p))
        m_ref[idx, :, q_lo:] = m_new

    @pl.when(i == 0)
    def _():
        for idx in range(n_maps):
            stage(sa_ref, mba_ref, idx, 0)

    acc_ref[...] = jnp.zeros_like(acc_ref)
    m_ref[...] = jnp.full_like(m_ref, NEG_BIG)

    def pair(j):
        for idx in range(n_maps):
            stage(sb_ref, mbb_ref, idx, j + 1)
            update(sa_ref[idx], idx, j, mba_ref[idx])
        for idx in range(n_maps):
            stage(sa_ref, mba_ref, idx, j + 2)
            update(sb_ref[idx], idx, j + 1, mbb_ref[idx])

    odd = i % 2

    @pl.when(odd == 1)
    def _():
        pair(0)

    @pl.loop(0, i // 2)
    def _(t):
        j = 2 * odd + 4 * t
        pair(j)
        pair(j + 2)

    for idx in range(n_maps):
        sb_ref[idx, :, half:] = jnp.where(visible[:, :half], scores(idx, n_off + 1, half), NEG_BIG)
        s = jnp.where(visible, sa_ref[idx], NEG_BIG)
        update(s, idx, n_off, colmax(s))
        stage(sa_ref, mba_ref, idx, 0, qn_ref)
    for h in range(DIFF_HEADS):
        o_maps = []
        for idx in (2 * h, 2 * h + 1):
            s = sb_ref[idx, :, half:]
            update(s, idx, n_off + 1, colmax(s), half)
            o_maps.append(acc_ref[idx, :pair_w, :] * (1.0 / acc_ref[idx, pair_w:pair_w + 1, :]))
        o = o_maps[0] - lam * o_maps[1]
        ms = jnp.mean(o * o, axis=0, keepdims=True)
        o_ref[rows(h), :] = o * lax.rsqrt(ms + EPS) * subg_ref[...] * out_scale


def _attn_call(lam, qm, k, vT, subg, *, out_scale):
    bsz, s_len, _ = k.shape
    bq, bk = ATTN_Q_BLOCK, ATTN_K_BLOCK
    assert bq == 2 * bk and s_len % bq == 0
    n_maps = 2 * DIFF_HEADS
    n_q = s_len // bq
    grid = (bsz, n_q)
    per_batch = lambda shape: pl.BlockSpec((None,) + shape, lambda b, i: (b, 0, 0))
    return pl.pallas_call(
        functools.partial(_attn_body, out_scale=out_scale),
        grid=grid,
        in_specs=[pl.BlockSpec(memory_space=pltpu.SMEM),
                  pl.BlockSpec((None, qm.shape[1], bq), lambda b, i: (b, 0, i)),
                  pl.BlockSpec((None, qm.shape[1], bq), lambda b, i: (b, 0, jnp.minimum(i + 1, n_q - 1))),
                  per_batch(k.shape[1:]), per_batch(vT.shape[1:]), _resident(subg)],
        out_specs=pl.BlockSpec((None, vT.shape[1], bq), lambda b, i: (b, 0, i)),
        out_shape=jax.ShapeDtypeStruct((bsz, vT.shape[1], s_len), F32),
        scratch_shapes=[pltpu.VMEM((n_maps, bk, bq), F32), pltpu.VMEM((n_maps, bk, bq), F32),
                        pltpu.VMEM((n_maps, 2 * HEAD_DIM + SUM_ROWS, bq), F32),
                        pltpu.VMEM((n_maps, 1, bq), F32), pltpu.VMEM((n_maps, 1, bq), F32),
                        pltpu.VMEM((n_maps, 1, bq), F32)],
        compiler_params=pltpu.CompilerParams(
            dimension_semantics=("arbitrary", "arbitrary"), vmem_limit_bytes=VMEM_LIMIT),
        name="diff_attn",
    )(lam, qm, qm, k, vT, subg)


def _ffn_chunks(d_ff):
    step = 1024
    return [(lo, min(lo + step, d_ff)) for lo in range(0, d_ff, step)]


def _ffn_body(yT_ref, part_ref, gb_ref, x_ref, wb1_ref, wo_ref, g1_ref, b1_ref, wgu_ref, wd_ref,
              g2_ref, b2_ref, o_ref, *, alpha, d_ff):
    tm = x_ref.shape[0]
    halves = [slice(lo, lo + FFN_ROWS) for lo in range(0, tm, FFN_ROWS)]
    y_b = yT_ref[...].T.astype(BF16)

    branch = [_dot(y_b[rs, :], wb1_ref[...]) for rs in halves]
    x1 = []
    for rs, br in zip(halves, branch):
        merged = part_ref[rs, :].astype(F32) + gb_ref[rs, :].astype(F32) * br
        mix = _dot(merged.astype(BF16), wo_ref[...])
        x1.append(_layer_norm(alpha * x_ref[rs, :] + mix, g1_ref[...], b1_ref[...]))
    for rs, x in zip(halves, x1):
        xb = x.astype(BF16)
        acc = alpha * x
        for lo, hi in _ffn_chunks(d_ff):
            gate = _dot(xb, wgu_ref[:, lo:hi])
            up = _dot(xb, wgu_ref[:, d_ff + lo:d_ff + hi])
            hidden = (gate * jax.nn.sigmoid(gate) * up).astype(BF16)
            acc = acc + _dot(hidden, wd_ref[lo:hi, :])
        o_ref[rs, :] = _layer_norm(acc, g2_ref[...], b2_ref[...])


def _ffn_call(yT, part, gb, x, layer, w_branch, w_o, g1, b1, w_gate_up, w_down, g2, b2, *, alpha):
    bsz, s_len, d_model = x.shape
    d_ff = w_down.shape[1]
    tm = min(FFN_TILE, s_len)
    tok = lambda w: pl.BlockSpec((None, tm, w), lambda b, t: (b, t, 0))
    weights = (w_branch, w_o, g1, b1, w_gate_up, w_down, g2, b2)
    weight_specs = [_resident(w_branch, layer, 1), _resident(w_o, layer), _resident(g1), _resident(b1),
                    _resident(w_gate_up, layer), _resident(w_down, layer), _resident(g2), _resident(b2)]
    return pl.pallas_call(
        functools.partial(_ffn_body, alpha=alpha, d_ff=d_ff),
        grid=(bsz, s_len // tm),
        in_specs=[pl.BlockSpec((None, yT.shape[1], tm), lambda b, t: (b, 0, t)),
                  tok(d_model), tok(d_model), tok(d_model)] + weight_specs,
        out_specs=tok(d_model),
        out_shape=jax.ShapeDtypeStruct((bsz, s_len, d_model), F32),
        compiler_params=pltpu.CompilerParams(
            dimension_semantics=("arbitrary", "arbitrary"), vmem_limit_bytes=VMEM_LIMIT),
        name="merge_swiglu",
    )(yT, part, gb, x, *weights)


def _rope_tables(positions):
    inv_freq = ROPE_THETA ** (-jnp.arange(0, ROPE_DIM, 2, dtype=F32) / ROPE_DIM)
    ang = positions.astype(F32)[:, None, :] * inv_freq[None, :, None]
    return jnp.cos(ang), jnp.sin(ang)


def kernel(x, positions, w_in, w_sgu, b_sgu, sgu_ln_g, sgu_ln_b, lambda_q1, lambda_k1, lambda_q2,
           lambda_k2, subln_g, conv_w, w_branch, w_o, ln1_g, ln1_b, w_gate_up, w_down, ln2_g, ln2_b):
    bsz, s_len, d_model = x.shape
    depth = w_in.shape[0]
    half = d_model // 2
    alpha = (2 * depth) ** 0.25
    rope_cos, rope_sin = _rope_tables(positions)
    tril = jnp.tril(jnp.ones((CHUNK, CHUNK), dtype=bool))
    row2 = lambda a: a.reshape(1, -1)
    w_in, w_branch, w_o, w_gate_up, w_down = (w.astype(BF16) for w in (w_in, w_branch, w_o, w_gate_up, w_down))
    w_sgu = jnp.where(tril, w_sgu, 0.0).astype(BF16)

    for l in range(depth):
        lambda_init = 0.8 - 0.6 * math.exp(-0.3 * l)
        lam = (jnp.exp(jnp.sum(lambda_q1[l] * lambda_k1[l])) - jnp.exp(jnp.sum(lambda_q2[l] * lambda_k2[l]))
               + lambda_init).reshape(1, 1).astype(F32)
        b_s = jnp.repeat(b_sgu[l].T, half // SGU_GROUPS, axis=1)
        qT, k, vT, part, gb = _in_proj_call(
            x, rope_cos, rope_sin, l, w_in, w_sgu, b_s, row2(sgu_ln_g[l]), row2(sgu_ln_b[l]),
            conv_w[l].T, w_branch)
        yT = _attn_call(lam, qT, k, vT, subln_g[l].reshape(-1, 1), out_scale=1.0 - lambda_init)
        x = _ffn_call(yT, part, gb, x, l, w_branch, w_o, row2(ln1_g[l]), row2(ln1_b[l]),
                      w_gate_up, w_down, row2(ln2_g[l]), row2(ln2_b[l]), alpha=alpha)
    return x
```

```python
import functools
import math

import jax
import jax.numpy as jnp
from jax import lax
from jax.experimental import pallas as pl
from jax.experimental.pallas import tpu as pltpu

F32 = jnp.float32
BF16 = jnp.bfloat16

CHUNK = 128
SGU_GROUPS = 4
DIFF_HEADS = 4
HEAD_DIM = 64
ROPE_DIM = HEAD_DIM // 4
ROPE_HALF = ROPE_DIM // 2
ROPE_THETA = 500000.0
CONV_K = 3
EPS = 1e-5
LOG2E = 1.4426950408889634
NEG_BIG = -1e30
SUM_ROWS = 16

PROJ_TILE = 1024
PROJ_ROWS = 256
ATTN_Q_BLOCK = 512
ATTN_K_BLOCK = 256
FFN_TILE = 1024
FFN_ROWS = 256
VMEM_LIMIT = 56 * 1024 * 1024


def _dot(a, b):
    return jnp.dot(a, b, preferred_element_type=F32)


def _layer_norm(x, g, b):
    mu = jnp.mean(x, axis=-1, keepdims=True)
    xc = x - mu
    var = jnp.mean(xc * xc, axis=-1, keepdims=True)
    return xc * lax.rsqrt(var + EPS) * g + b


def _resident(arr, *lead):
    tail = arr.shape[len(lead):]
    return pl.BlockSpec((None,) * len(lead) + tail, lambda *_: lead + (0,) * len(tail),
                        pipeline_mode=pl.Buffered(1))


def _in_proj_body(x_ref, cos_ref, sin_ref, w_in_ref, wsgu_ref, bsgu_ref, lng_ref, lnb_ref,
                  convw_ref, wb0_ref, wb2_ref,
                  qT_ref, k_ref, vT_ref, part_ref, gb_ref, carry_ref, *, d_model):
    tm = x_ref.shape[0]
    half = d_model // 2
    o1 = d_model
    qk_w = 2 * DIFF_HEADS * HEAD_DIM
    v_w = DIFF_HEADS * 2 * HEAD_DIM
    o3 = o1 + 2 * qk_w
    o4 = o3 + v_w
    o5 = o4 + 3 * half

    def piece(rs, prev):
        rows = rs.stop - rs.start
        xb = x_ref[rs, :].astype(BF16)

        def proj(lo, hi):
            return _dot(xb, w_in_ref[:, lo:hi])

        za = proj(0, d_model)
        zq_t = proj(o1, o1 + qk_w).T
        zk_t = proj(o1 + qk_w, o3).T

        za = 0.5 * za * (1.0 + lax.erf(za * (1.0 / math.sqrt(2.0))))
        u = za[:, :half]
        v = _layer_norm(za[:, half:], lng_ref[...], lnb_ref[...]).astype(BF16)

        cos, sin = cos_ref[:, rs], sin_ref[:, rs]

        def rope_t(z_t):
            pieces = []
            for lo in range(0, qk_w, HEAD_DIM):
                r1, r2 = z_t[lo:lo + ROPE_HALF], z_t[lo + ROPE_HALF:lo + ROPE_DIM]
                pieces += [r1 * cos - r2 * sin, r2 * cos + r1 * sin, z_t[lo + ROPE_DIM:lo + HEAD_DIM]]
            return jnp.concatenate(pieces, axis=0)

        q_t = (rope_t(zq_t) * (HEAD_DIM ** -0.5 * LOG2E)).astype(BF16)
        no_q = jnp.zeros((HEAD_DIM, rows), BF16)
        maps = []
        for lo in range(0, qk_w, 2 * HEAD_DIM):
            maps += [q_t[lo:lo + HEAD_DIM], no_q, no_q, q_t[lo + HEAD_DIM:lo + 2 * HEAD_DIM]]
        qT_ref[:, rs] = jnp.concatenate(maps, axis=0)
        k_ref[rs, :] = rope_t(zk_t).T.astype(BF16)
        vT_ref[:, rs] = proj(o3, o4).T.astype(BF16)
        gb_ref[rs, :] = jax.nn.sigmoid(proj(o5 + d_model, o5 + 2 * d_model)).astype(BF16)

        gw = half // SGU_GROUPS
        mixed = []
        for c in range(rows // CHUNK):
            cols = [_dot(wsgu_ref[g], v[c * CHUNK:(c + 1) * CHUNK, g * gw:(g + 1) * gw])
                    for g in range(SGU_GROUPS)]
            mixed.append(jnp.concatenate(cols, axis=1) + bsgu_ref[...])
        y_a = (u * jnp.concatenate(mixed, axis=0)).astype(BF16)

        zc = proj(o4, o5)
        h = zc[:, half:2 * half] * zc[:, 2 * half:]
        row = lax.broadcasted_iota(jnp.int32, (rows, half), 0)
        h1 = jnp.where(row == 0, prev[7:8, :], pltpu.roll(h, 1, 0))
        h2 = jnp.where(row == 0, prev[6:7, :], jnp.where(row == 1, prev[7:8, :], pltpu.roll(h, 2, 0)))
        conv = h2 * convw_ref[0:1, :] + h1 * convw_ref[1:2, :] + h * convw_ref[2:3, :]
        y_c = (zc[:, :half] * conv).astype(BF16)

        g_a = jax.nn.sigmoid(proj(o5, o5 + d_model))
        g_c = jax.nn.sigmoid(proj(o5 + 2 * d_model, o5 + 3 * d_model))
        part_ref[rs, :] = (g_a * _dot(y_a, wb0_ref[...]) + g_c * _dot(y_c, wb2_ref[...])).astype(BF16)
        return h[rows - 8:, :]

    @pl.when(pl.program_id(1) == 0)
    def _():
        carry_ref[...] = jnp.zeros_like(carry_ref)

    prev = carry_ref[...]
    for lo in range(0, tm, PROJ_ROWS):
        prev = piece(slice(lo, lo + PROJ_ROWS), prev)
    carry_ref[...] = prev


def _in_proj_call(x, rope_cos, rope_sin, layer, w_in, w_sgu, b_sgu, ln_g, ln_b, conv_w, w_branch):
    bsz, s_len, d_model = x.shape
    tm = min(PROJ_TILE, s_len)
    half = d_model // 2
    qk_w = 2 * DIFF_HEADS * HEAD_DIM
    v_w = DIFF_HEADS * 2 * HEAD_DIM
    grid = (bsz, s_len // tm)
    tok = lambda w: pl.BlockSpec((None, tm, w), lambda b, t: (b, t, 0))
    tokT = lambda w: pl.BlockSpec((None, w, tm), lambda b, t: (b, 0, t))
    return pl.pallas_call(
        functools.partial(_in_proj_body, d_model=d_model),
        grid=grid,
        in_specs=[tok(d_model), tokT(ROPE_HALF), tokT(ROPE_HALF),
                  _resident(w_in, layer), _resident(w_sgu, layer), _resident(b_sgu),
                  _resident(ln_g), _resident(ln_b), _resident(conv_w),
                  _resident(w_branch, layer, 0), _resident(w_branch, layer, 2)],
        out_specs=[tokT(2 * qk_w), tok(qk_w), tokT(v_w), tok(d_model), tok(d_model)],
        out_shape=[jax.ShapeDtypeStruct((bsz, 2 * qk_w, s_len), BF16),
                   jax.ShapeDtypeStruct((bsz, s_len, qk_w), BF16),
                   jax.ShapeDtypeStruct((bsz, v_w, s_len), BF16),
                   jax.ShapeDtypeStruct((bsz, s_len, d_model), BF16),
                   jax.ShapeDtypeStruct((bsz, s_len, d_model), BF16)],
        scratch_shapes=[pltpu.VMEM((8, half), F32)],
        compiler_params=pltpu.CompilerParams(
            dimension_semantics=("arbitrary", "arbitrary"), vmem_limit_bytes=VMEM_LIMIT),
        name="in_proj",
    )(x, rope_cos, rope_sin, w_in, w_sgu, b_sgu, ln_g, ln_b, conv_w, w_branch, w_branch)


def _attn_body(lam_ref, qm_ref, qn_ref, k_ref, vT_ref, subg_ref, o_ref, sa_ref, sb_ref, acc_ref, m_ref,
               mba_ref, mbb_ref, *, out_scale):
    n_maps, bk, bq = sa_ref.shape
    assert bq == 2 * bk and n_maps == 2 * DIFF_HEADS
    half = bq // 2
    i = pl.program_id(1)
    n_off = 2 * i
    lam = lam_ref[0, 0]
    pair_w = 2 * HEAD_DIM
    visible = (lax.broadcasted_iota(jnp.int32, (bk, bq), 0)
               <= lax.broadcasted_iota(jnp.int32, (bk, bq), 1))

    def colmax(s):
        return jnp.max(s, axis=0, keepdims=True)

    def rows(r):
        return slice(r * pair_w, (r + 1) * pair_w)

    def keys(kb):
        return pl.ds(pl.multiple_of(kb * bk, bk), bk)

    def values(h, kb):
        return jnp.concatenate([vT_ref[rows(h), keys(kb)], jnp.ones((SUM_ROWS, bk), BF16)], axis=0)

    def scores(idx, kb, q_lo=0, q_ref=qm_ref):
        return _dot(k_ref[keys(kb), rows(idx // 2)], q_ref[rows(idx), q_lo:])

    def stage(s_ref, mb_ref, idx, kb, q_ref=qm_ref):
        s = scores(idx, kb, 0, q_ref)
        s_ref[idx] = s
        mb_ref[idx] = colmax(s)

    def reset(idx):
        acc_ref[idx] = jnp.zeros(acc_ref.shape[1:], F32)
        m_ref[idx] = jnp.full(m_ref.shape[1:], NEG_BIG, F32)

    def update(s, idx, kb, mb, q_lo=0):
        m_run = m_ref[idx, :, q_lo:]
        m_new = jnp.maximum(m_run, mb)
        p = jnp.exp2(s - m_new).astype(BF16)
        acc_ref[idx, :, q_lo:] = (jnp.exp2(m_run - m_new) * acc_ref[idx, :, q_lo:]
                                  + _dot(values(idx // 2, kb), p))
        m_ref[idx, :, q_lo:] = m_new

    @pl.when(i == 0)
    def _():
        for idx in range(n_maps):
            stage(sa_ref, mba_ref, idx, 0)
        for idx in range(n_maps):
            reset(idx)

    def pair(j):
        for idx in range(n_maps):
            stage(sb_ref, mbb_ref, idx, j + 1)
            update(sa_ref[idx], idx, j, mba_ref[idx])
        for idx in range(n_maps):
            stage(sa_ref, mba_ref, idx, j + 2)
            update(sb_ref[idx], idx, j + 1, mbb_ref[idx])

    odd = i % 2

    @pl.when(odd == 1)
    def _():
        pair(0)

    @pl.loop(0, i // 2)
    def _(t):
        j = 2 * odd + 4 * t
        pair(j)
        pair(j + 2)

    tri = visible[:, :half]
    for idx in range(n_maps):
        sb_ref[idx, :, half:] = jnp.where(tri, scores(idx, n_off + 1, half), NEG_BIG)
        s_lo = jnp.where(tri, sa_ref[idx, :, :half], NEG_BIG)
        s = jnp.concatenate([s_lo, sa_ref[idx, :, half:]], axis=1)
        mb = jnp.concatenate([colmax(s_lo), mba_ref[idx, :, half:]], axis=1)
        update(s, idx, n_off, mb)
        stage(sa_ref, mba_ref, idx, 0, qn_ref)
    for h in range(DIFF_HEADS):
        o_maps = []
        for idx in (2 * h, 2 * h + 1):
            s = sb_ref[idx, :, half:]
            update(s, idx, n_off + 1, colmax(s), half)
            o_maps.append((acc_ref[idx, :pair_w, :], 1.0 / acc_ref[idx, pair_w:pair_w + 1, :]))
        (o0, r0), (o1, r1) = o_maps
        o = o0 * r0 - o1 * (lam * r1)
        reset(2 * h)
        reset(2 * h + 1)
        ms = jnp.mean(o * o, axis=0, keepdims=True)
        o_ref[rows(h), :] = o * lax.rsqrt(ms + EPS) * (subg_ref[...] * out_scale)


def _attn_call(lam, qm, k, vT, subg, *, out_scale):
    bsz, s_len, _ = k.shape
    bq, bk = ATTN_Q_BLOCK, ATTN_K_BLOCK
    assert bq == 2 * bk and s_len % bq == 0
    n_maps = 2 * DIFF_HEADS
    n_q = s_len // bq
    grid = (bsz, n_q)
    per_batch = lambda shape: pl.BlockSpec((None,) + shape, lambda b, i: (b, 0, 0))
    return pl.pallas_call(
        functools.partial(_attn_body, out_scale=out_scale),
        grid=grid,
        in_specs=[pl.BlockSpec(memory_space=pltpu.SMEM),
                  pl.BlockSpec((None, qm.shape[1], bq), lambda b, i: (b, 0, i)),
                  pl.BlockSpec((None, qm.shape[1], bq), lambda b, i: (b, 0, jnp.minimum(i + 1, n_q - 1))),
                  per_batch(k.shape[1:]), per_batch(vT.shape[1:]), _resident(subg)],
        out_specs=pl.BlockSpec((None, vT.shape[1], bq), lambda b, i: (b, 0, i)),
        out_shape=jax.ShapeDtypeStruct((bsz, vT.shape[1], s_len), F32),
        scratch_shapes=[pltpu.VMEM((n_maps, bk, bq), F32), pltpu.VMEM((n_maps, bk, bq), F32),
                        pltpu.VMEM((n_maps, 2 * HEAD_DIM + SUM_ROWS, bq), F32),
                        pltpu.VMEM((n_maps, 1, bq), F32), pltpu.VMEM((n_maps, 1, bq), F32),
                        pltpu.VMEM((n_maps, 1, bq), F32)],
        compiler_params=pltpu.CompilerParams(
            dimension_semantics=("arbitrary", "arbitrary"), vmem_limit_bytes=VMEM_LIMIT),
        name="diff_attn",
    )(lam, qm, qm, k, vT, subg)


def _ffn_chunks(d_ff):
    step = 1024
    return [(lo, min(lo + step, d_ff)) for lo in range(0, d_ff, step)]


def _ffn_body(yT_ref, part_ref, gb_ref, x_ref, wb1_ref, wo_ref, g1_ref, b1_ref, wgu_ref, wd_ref,
              g2_ref, b2_ref, o_ref, *, alpha, d_ff):
    tm = x_ref.shape[0]
    halves = [slice(lo, lo + FFN_ROWS) for lo in range(0, tm, FFN_ROWS)]
    y_b = yT_ref[...].T.astype(BF16)

    branch = [_dot(y_b[rs, :], wb1_ref[...]) for rs in halves]
    x1 = []
    for rs, br in zip(halves, branch):
        merged = part_ref[rs, :].astype(F32) + gb_ref[rs, :].astype(F32) * br
        mix = _dot(merged.astype(BF16), wo_ref[...])
        x1.append(_layer_norm(alpha * x_ref[rs, :] + mix, g1_ref[...], b1_ref[...]))
    for rs, x in zip(halves, x1):
        xb = x.astype(BF16)
        acc = alpha * x
        for lo, hi in _ffn_chunks(d_ff):
            gate = _dot(xb, wgu_ref[:, lo:hi])
            up = _dot(xb, wgu_ref[:, d_ff + lo:d_ff + hi])
            hidden = (gate * jax.nn.sigmoid(gate) * up).astype(BF16)
            acc = acc + _dot(hidden, wd_ref[lo:hi, :])
        o_ref[rs, :] = _layer_norm(acc, g2_ref[...], b2_ref[...])


def _ffn_call(yT, part, gb, x, layer, w_branch, w_o, g1, b1, w_gate_up, w_down, g2, b2, *, alpha):
    bsz, s_len, d_model = x.shape
    d_ff = w_down.shape[1]
    tm = min(FFN_TILE, s_len)
    tok = lambda w: pl.BlockSpec((None, tm, w), lambda b, t: (b, t, 0))
    weights = (w_branch, w_o, g1, b1, w_gate_up, w_down, g2, b2)
    weight_specs = [_resident(w_branch, layer, 1), _resident(w_o, layer), _resident(g1), _resident(b1),
                    _resident(w_gate_up, layer), _resident(w_down, layer), _resident(g2), _resident(b2)]
    return pl.pallas_call(
        functools.partial(_ffn_body, alpha=alpha, d_ff=d_ff),
        grid=(bsz, s_len // tm),
        in_specs=[pl.BlockSpec((None, yT.shape[1], tm), lambda b, t: (b, 0, t)),
                  tok(d_model), tok(d_model), tok(d_model)] + weight_specs,
        out_specs=tok(d_model),
        out_shape=jax.ShapeDtypeStruct((bsz, s_len, d_model), F32),
        compiler_params=pltpu.CompilerParams(
            dimension_semantics=("arbitrary", "arbitrary"), vmem_limit_bytes=VMEM_LIMIT),
        name="merge_swiglu",
    )(yT, part, gb, x, *weights)


def _rope_tables(positions):
    inv_freq = ROPE_THETA ** (-jnp.arange(0, ROPE_DIM, 2, dtype=F32) / ROPE_DIM)
    ang = positions.astype(F32)[:, None, :] * inv_freq[None, :, None]
    return jnp.cos(ang), jnp.sin(ang)


def kernel(x, positions, w_in, w_sgu, b_sgu, sgu_ln_g, sgu_ln_b, lambda_q1, lambda_k1, lambda_q2,
           lambda_k2, subln_g, conv_w, w_branch, w_o, ln1_g, ln1_b, w_gate_up, w_down, ln2_g, ln2_b):
    bsz, s_len, d_model = x.shape
    depth = w_in.shape[0]
    half = d_model // 2
    alpha = (2 * depth) ** 0.25
    rope_cos, rope_sin = _rope_tables(positions)
    tril = jnp.tril(jnp.ones((CHUNK, CHUNK), dtype=bool))
    row2 = lambda a: a.reshape(1, -1)
    w_in, w_branch, w_o, w_gate_up, w_down = (w.astype(BF16) for w in (w_in, w_branch, w_o, w_gate_up, w_down))
    w_sgu = jnp.where(tril, w_sgu, 0.0).astype(BF16)

    for l in range(depth):
        lambda_init = 0.8 - 0.6 * math.exp(-0.3 * l)
        lam = (jnp.exp(jnp.sum(lambda_q1[l] * lambda_k1[l])) - jnp.exp(jnp.sum(lambda_q2[l] * lambda_k2[l]))
               + lambda_init).reshape(1, 1).astype(F32)
        b_s = jnp.repeat(b_sgu[l].T, half // SGU_GROUPS, axis=1)
        qT, k, vT, part, gb = _in_proj_call(
            x, rope_cos, rope_sin, l, w_in, w_sgu, b_s, row2(sgu_ln_g[l]), row2(sgu_ln_b[l]),
            conv_w[l].T, w_branch)
        yT = _attn_call(lam, qT, k, vT, subln_g[l].reshape(-1, 1), out_scale=1.0 - lambda_init)
        x = _ffn_call(yT, part, gb, x, l, w_branch, w_o, row2(ln1_g[l]), row2(ln1_b[l]),
                      w_gate_up, w_down, row2(ln2_g[l]), row2(ln2_b[l]), alpha=alpha)
    return x
```

```python
import functools
import math

import jax
import jax.numpy as jnp
from jax import lax
from jax.experimental import pallas as pl
from jax.experimental.pallas import tpu as pltpu

F32 = jnp.float32
BF16 = jnp.bfloat16

CHUNK = 128
SGU_GROUPS = 4
DIFF_HEADS = 4
HEAD_DIM = 64
ROPE_DIM = HEAD_DIM // 4
ROPE_HALF = ROPE_DIM // 2
ROPE_THETA = 500000.0
CONV_K = 3
EPS = 1e-5
LOG2E = 1.4426950408889634
NEG_BIG = -1e30
SUM_ROWS = 16

PROJ_TILE = 1024
PROJ_ROWS = 256
ATTN_Q_BLOCK = 512
ATTN_K_BLOCK = 256
FFN_TILE = 1024
FFN_ROWS = 256
VMEM_LIMIT = 56 * 1024 * 1024


def _dot(a, b):
    return jnp.dot(a, b, preferred_element_type=F32)


def _layer_norm(x, g, b):
    mu = jnp.mean(x, axis=-1, keepdims=True)
    xc = x - mu
    var = jnp.mean(xc * xc, axis=-1, keepdims=True)
    return xc * lax.rsqrt(var + EPS) * g + b


def _resident(arr, *lead):
    tail = arr.shape[len(lead):]
    return pl.BlockSpec((None,) * len(lead) + tail, lambda *_: lead + (0,) * len(tail),
                        pipeline_mode=pl.Buffered(1))


def _in_proj_body(x_ref, cos_ref, sin_ref, w_in_ref, wsgu_ref, bsgu_ref, lng_ref, lnb_ref,
                  convw_ref, wb0_ref, wb2_ref,
                  qT_ref, k_ref, vT_ref, part_ref, gb_ref, carry_ref, *, d_model):
    tm = x_ref.shape[0]
    half = d_model // 2
    o1 = d_model
    qk_w = 2 * DIFF_HEADS * HEAD_DIM
    v_w = DIFF_HEADS * 2 * HEAD_DIM
    o3 = o1 + 2 * qk_w
    o4 = o3 + v_w
    o5 = o4 + 3 * half

    def piece(rs, prev):
        rows = rs.stop - rs.start
        xb = x_ref[rs, :].astype(BF16)

        def proj(lo, hi):
            return _dot(xb, w_in_ref[:, lo:hi])

        za = proj(0, d_model)
        zq_t = proj(o1, o1 + qk_w).T
        zk_t = proj(o1 + qk_w, o3).T

        za = 0.5 * za * (1.0 + lax.erf(za * (1.0 / math.sqrt(2.0))))
        u = za[:, :half]
        v = _layer_norm(za[:, half:], lng_ref[...], lnb_ref[...]).astype(BF16)

        cos, sin = cos_ref[:, rs], sin_ref[:, rs]

        def rope_t(z_t):
            pieces = []
            for lo in range(0, qk_w, HEAD_DIM):
                r1, r2 = z_t[lo:lo + ROPE_HALF], z_t[lo + ROPE_HALF:lo + ROPE_DIM]
                pieces += [r1 * cos - r2 * sin, r2 * cos + r1 * sin, z_t[lo + ROPE_DIM:lo + HEAD_DIM]]
            return jnp.concatenate(pieces, axis=0)

        q_t = (rope_t(zq_t) * (HEAD_DIM ** -0.5 * LOG2E)).astype(BF16)
        no_q = jnp.zeros((HEAD_DIM, rows), BF16)
        maps = []
        for lo in range(0, qk_w, 2 * HEAD_DIM):
            maps += [q_t[lo:lo + HEAD_DIM], no_q, no_q, q_t[lo + HEAD_DIM:lo + 2 * HEAD_DIM]]
        qT_ref[:, rs] = jnp.concatenate(maps, axis=0)
        k_ref[rs, :] = rope_t(zk_t).T.astype(BF16)
        vT_ref[:, rs] = proj(o3, o4).T.astype(BF16)
        gb_ref[rs, :] = jax.nn.sigmoid(proj(o5 + d_model, o5 + 2 * d_model)).astype(BF16)

        gw = half // SGU_GROUPS
        mixed = []
        for c in range(rows // CHUNK):
            cols = [_dot(wsgu_ref[g], v[c * CHUNK:(c + 1) * CHUNK, g * gw:(g + 1) * gw])
                    for g in range(SGU_GROUPS)]
            mixed.append(jnp.concatenate(cols, axis=1) + bsgu_ref[...])
        y_a = (u * jnp.concatenate(mixed, axis=0)).astype(BF16)

        zc = proj(o4, o5)
        h = zc[:, half:2 * half] * zc[:, 2 * half:]
        row = lax.broadcasted_iota(jnp.int32, (rows, half), 0)
        h1 = jnp.where(row == 0, prev[7:8, :], pltpu.roll(h, 1, 0))
        h2 = jnp.where(row == 0, prev[6:7, :], jnp.where(row == 1, prev[7:8, :], pltpu.roll(h, 2, 0)))
        conv = h2 * convw_ref[0:1, :] + h1 * convw_ref[1:2, :] + h * convw_ref[2:3, :]
        y_c = (zc[:, :half] * conv).astype(BF16)

        g_a = jax.nn.sigmoid(proj(o5, o5 + d_model))
        g_c = jax.nn.sigmoid(proj(o5 + 2 * d_model, o5 + 3 * d_model))
        part_ref[rs, :] = (g_a * _dot(y_a, wb0_ref[...]) + g_c * _dot(y_c, wb2_ref[...])).astype(BF16)
        return h[rows - 8:, :]

    @pl.when(pl.program_id(1) == 0)
    def _():
        carry_ref[...] = jnp.zeros_like(carry_ref)

    prev = carry_ref[...]
    for lo in range(0, tm, PROJ_ROWS):
        prev = piece(slice(lo, lo + PROJ_ROWS), prev)
    carry_ref[...] = prev


def _in_proj_call(x, rope_cos, rope_sin, layer, w_in, w_sgu, b_sgu, ln_g, ln_b, conv_w, w_branch):
    assert conv_w.shape[0] == CONV_K == 3
    bsz, s_len, d_model = x.shape
    tm = min(PROJ_TILE, s_len)
    half = d_model // 2
    qk_w = 2 * DIFF_HEADS * HEAD_DIM
    v_w = DIFF_HEADS * 2 * HEAD_DIM
    grid = (bsz, s_len // tm)
    tok = lambda w: pl.BlockSpec((None, tm, w), lambda b, t: (b, t, 0))
    tokT = lambda w: pl.BlockSpec((None, w, tm), lambda b, t: (b, 0, t))
    return pl.pallas_call(
        functools.partial(_in_proj_body, d_model=d_model),
        grid=grid,
        in_specs=[tok(d_model), tokT(ROPE_HALF), tokT(ROPE_HALF),
                  _resident(w_in, layer), _resident(w_sgu, layer), _resident(b_sgu),
                  _resident(ln_g), _resident(ln_b), _resident(conv_w),
                  _resident(w_branch, layer, 0), _resident(w_branch, layer, 2)],
        out_specs=[tokT(2 * qk_w), tok(qk_w), tokT(v_w), tok(d_model), tok(d_model)],
        out_shape=[jax.ShapeDtypeStruct((bsz, 2 * qk_w, s_len), BF16),
                   jax.ShapeDtypeStruct((bsz, s_len, qk_w), BF16),
                   jax.ShapeDtypeStruct((bsz, v_w, s_len), BF16),
                   jax.ShapeDtypeStruct((bsz, s_len, d_model), BF16),
                   jax.ShapeDtypeStruct((bsz, s_len, d_model), BF16)],
        scratch_shapes=[pltpu.VMEM((8, half), F32)],
        compiler_params=pltpu.CompilerParams(
            dimension_semantics=("arbitrary", "arbitrary"), vmem_limit_bytes=VMEM_LIMIT),
        name="in_proj",
    )(x, rope_cos, rope_sin, w_in, w_sgu, b_sgu, ln_g, ln_b, conv_w, w_branch, w_branch)


def _attn_body(lam_ref, qm_ref, qn_ref, k_ref, vT_ref, subg_ref, o_ref, sa_ref, sb_ref, acc_ref, m_ref,
               mba_ref, mbb_ref, *, out_scale):
    n_maps, bk, bq = sa_ref.shape
    assert bq == 2 * bk and n_maps == 2 * DIFF_HEADS
    half = bq // 2
    i = pl.program_id(1)
    n_off = 2 * i
    lam = lam_ref[0, 0]
    pair_w = 2 * HEAD_DIM
    visible = (lax.broadcasted_iota(jnp.int32, (bk, bq), 0)
               <= lax.broadcasted_iota(jnp.int32, (bk, bq), 1))

    def colmax(s):
        return jnp.max(s, axis=0, keepdims=True)

    def rows(r):
        return slice(r * pair_w, (r + 1) * pair_w)

    def keys(kb):
        return pl.ds(pl.multiple_of(kb * bk, bk), bk)

    def values(h, kb):
        return jnp.concatenate([vT_ref[rows(h), keys(kb)], jnp.ones((SUM_ROWS, bk), BF16)], axis=0)

    def scores(idx, kb, q_lo=0, q_ref=qm_ref):
        return _dot(k_ref[keys(kb), rows(idx // 2)], q_ref[rows(idx), q_lo:])

    def stage(s_ref, mb_ref, idx, kb, q_ref=qm_ref):
        s = scores(idx, kb, 0, q_ref)
        s_ref[idx] = s
        mb_ref[idx] = colmax(s)

    def reset(idx):
        acc_ref[idx] = jnp.zeros(acc_ref.shape[1:], F32)
        m_ref[idx] = jnp.full(m_ref.shape[1:], NEG_BIG, F32)

    def update(s, idx, kb, mb, q_lo=0):
        m_run = m_ref[idx, :, q_lo:]
        m_new = jnp.maximum(m_run, mb)
        p = jnp.exp2(s - m_new).astype(BF16)
        acc_ref[idx, :, q_lo:] = (jnp.exp2(m_run - m_new) * acc_ref[idx, :, q_lo:]
                                  + _dot(values(idx // 2, kb), p))
        m_ref[idx, :, q_lo:] = m_new

    @pl.when(i == 0)
    def _():
        for idx in range(n_maps):
            stage(sa_ref, mba_ref, idx, 0)
        for idx in range(n_maps):
            reset(idx)

    def pair(j):
        for idx in range(n_maps):
            stage(sb_ref, mbb_ref, idx, j + 1)
            update(sa_ref[idx], idx, j, mba_ref[idx])
        for idx in range(n_maps):
            stage(sa_ref, mba_ref, idx, j + 2)
            update(sb_ref[idx], idx, j + 1, mbb_ref[idx])

    odd = i % 2

    @pl.when(odd == 1)
    def _():
        pair(0)

    @pl.loop(0, i // 2)
    def _(t):
        j = 2 * odd + 4 * t
        pair(j)
        pair(j + 2)

    tri = visible[:, :half]
    for idx in range(n_maps):
        sb_ref[idx, :, half:] = jnp.where(tri, scores(idx, n_off + 1, half), NEG_BIG)
        s_lo = jnp.where(tri, sa_ref[idx, :, :half], NEG_BIG)
        s = jnp.concatenate([s_lo, sa_ref[idx, :, half:]], axis=1)
        mb = jnp.concatenate([colmax(s_lo), mba_ref[idx, :, half:]], axis=1)
        update(s, idx, n_off, mb)
        stage(sa_ref, mba_ref, idx, 0, qn_ref)
    for h in range(DIFF_HEADS):
        o_maps = []
        for idx in (2 * h, 2 * h + 1):
            s = sb_ref[idx, :, half:]
            update(s, idx, n_off + 1, colmax(s), half)
            o_maps.append((acc_ref[idx, :pair_w, :], 1.0 / acc_ref[idx, pair_w:pair_w + 1, :]))
        (o0, r0), (o1, r1) = o_maps
        o = o0 * r0 - o1 * (lam * r1)
        reset(2 * h)
        reset(2 * h + 1)
        ms = jnp.mean(o * o, axis=0, keepdims=True)
        o_ref[rows(h), :] = o * lax.rsqrt(ms + EPS) * (subg_ref[...] * out_scale)


def _attn_call(lam, qm, k, vT, subg, *, out_scale):
    bsz, s_len, _ = k.shape
    bq, bk = ATTN_Q_BLOCK, ATTN_K_BLOCK
    assert bq == 2 * bk and s_len % bq == 0
    n_maps = 2 * DIFF_HEADS
    n_q = s_len // bq
    grid = (bsz, n_q)
    per_batch = lambda shape: pl.BlockSpec((None,) + shape, lambda b, i: (b, 0, 0))
    return pl.pallas_call(
        functools.partial(_attn_body, out_scale=out_scale),
        grid=grid,
        in_specs=[pl.BlockSpec(memory_space=pltpu.SMEM),
                  pl.BlockSpec((None, qm.shape[1], bq), lambda b, i: (b, 0, i)),
                  pl.BlockSpec((None, qm.shape[1], bq), lambda b, i: (b, 0, jnp.minimum(i + 1, n_q - 1))),
                  per_batch(k.shape[1:]), per_batch(vT.shape[1:]), _resident(subg)],
        out_specs=pl.BlockSpec((None, vT.shape[1], bq), lambda b, i: (b, 0, i)),
        out_shape=jax.ShapeDtypeStruct((bsz, vT.shape[1], s_len), F32),
        scratch_shapes=[pltpu.VMEM((n_maps, bk, bq), F32), pltpu.VMEM((n_maps, bk, bq), F32),
                        pltpu.VMEM((n_maps, 2 * HEAD_DIM + SUM_ROWS, bq), F32),
                        pltpu.VMEM((n_maps, 1, bq), F32), pltpu.VMEM((n_maps, 1, bq), F32),
                        pltpu.VMEM((n_maps, 1, bq), F32)],
        compiler_params=pltpu.CompilerParams(
            dimension_semantics=("arbitrary", "arbitrary"), vmem_limit_bytes=VMEM_LIMIT),
        name="diff_attn",
    )(lam, qm, qm, k, vT, subg)


def _ffn_chunks(d_ff):
    step = 1024
    return [(lo, min(lo + step, d_ff)) for lo in range(0, d_ff, step)]


def _ffn_body(yT_ref, part_ref, gb_ref, x_ref, wb1_ref, wo_ref, g1_ref, b1_ref, wgu_ref, wd_ref,
              g2_ref, b2_ref, o_ref, *, alpha, d_ff):
    tm = x_ref.shape[0]
    pieces = [slice(lo, lo + FFN_ROWS) for lo in range(0, tm, FFN_ROWS)]
    y_b = yT_ref[...].T.astype(BF16)

    branch = [_dot(y_b[rs, :], wb1_ref[...]) for rs in pieces]
    x1 = []
    for rs, br in zip(pieces, branch):
        merged = part_ref[rs, :].astype(F32) + gb_ref[rs, :].astype(F32) * br
        mix = _dot(merged.astype(BF16), wo_ref[...])
        x1.append(_layer_norm(alpha * x_ref[rs, :] + mix, g1_ref[...], b1_ref[...]))
    for rs, x in zip(pieces, x1):
        xb = x.astype(BF16)
        acc = alpha * x
        for lo, hi in _ffn_chunks(d_ff):
            gate = _dot(xb, wgu_ref[:, lo:hi])
            up = _dot(xb, wgu_ref[:, d_ff + lo:d_ff + hi])
            hidden = (gate * jax.nn.sigmoid(gate) * up).astype(BF16)
            acc = acc + _dot(hidden, wd_ref[lo:hi, :])
        o_ref[rs, :] = _layer_norm(acc, g2_ref[...], b2_ref[...])


def _ffn_call(yT, part, gb, x, layer, w_branch, w_o, g1, b1, w_gate_up, w_down, g2, b2, *, alpha):
    bsz, s_len, d_model = x.shape
    d_ff = w_down.shape[1]
    tm = min(FFN_TILE, s_len)
    tok = lambda w: pl.BlockSpec((None, tm, w), lambda b, t: (b, t, 0))
    weights = (w_branch, w_o, g1, b1, w_gate_up, w_down, g2, b2)
    weight_specs = [_resident(w_branch, layer, 1), _resident(w_o, layer), _resident(g1), _resident(b1),
                    _resident(w_gate_up, layer), _resident(w_down, layer), _resident(g2), _resident(b2)]
    return pl.pallas_call(
        functools.partial(_ffn_body, alpha=alpha, d_ff=d_ff),
        grid=(bsz, s_len // tm),
        in_specs=[pl.BlockSpec((None, yT.shape[1], tm), lambda b, t: (b, 0, t)),
                  tok(d_model), tok(d_model), tok(d_model)] + weight_specs,
        out_specs=tok(d_model),
        out_shape=jax.ShapeDtypeStruct((bsz, s_len, d_model), F32),
        compiler_params=pltpu.CompilerParams(
            dimension_semantics=("arbitrary", "arbitrary"), vmem_limit_bytes=VMEM_LIMIT),
        name="merge_swiglu",
    )(yT, part, gb, x, *weights)


def _rope_tables(positions):
    inv_freq = ROPE_THETA ** (-jnp.arange(0, ROPE_DIM, 2, dtype=F32) / ROPE_DIM)
    ang = positions.astype(F32)[:, None, :] * inv_freq[None, :, None]
    return jnp.cos(ang), jnp.sin(ang)


def kernel(x, positions, w_in, w_sgu, b_sgu, sgu_ln_g, sgu_ln_b, lambda_q1, lambda_k1, lambda_q2,
           lambda_k2, subln_g, conv_w, w_branch, w_o, ln1_g, ln1_b, w_gate_up, w_down, ln2_g, ln2_b):
    bsz, s_len, d_model = x.shape
    depth = w_in.shape[0]
    half = d_model // 2
    alpha = (2 * depth) ** 0.25
    rope_cos, rope_sin = _rope_tables(positions)
    tril = jnp.tril(jnp.ones((CHUNK, CHUNK), dtype=bool))
    row2 = lambda a: a.reshape(1, -1)
    w_in, w_branch, w_o, w_gate_up, w_down = (w.astype(BF16) for w in (w_in, w_branch, w_o, w_gate_up, w_down))
    w_sgu = jnp.where(tril, w_sgu, 0.0).astype(BF16)

    for l in range(depth):
        lambda_init = 0.8 - 0.6 * math.exp(-0.3 * l)
        lam = (jnp.exp(jnp.sum(lambda_q1[l] * lambda_k1[l])) - jnp.exp(jnp.sum(lambda_q2[l] * lambda_k2[l]))
               + lambda_init).reshape(1, 1).astype(F32)
        b_s = jnp.repeat(b_sgu[l].T, half // SGU_GROUPS, axis=1)
        qT, k, vT, part, gb = _in_proj_call(
            x, rope_cos, rope_sin, l, w_in, w_sgu, b_s, row2(sgu_ln_g[l]), row2(sgu_ln_b[l]),
            conv_w[l].T, w_branch)
        yT = _attn_call(lam, qT, k, vT, subln_g[l].reshape(-1, 1), out_scale=1.0 - lambda_init)
        x = _ffn_call(yT, part, gb, x, l, w_branch, w_o, row2(ln1_g[l]), row2(ln1_b[l]),
                      w_gate_up, w_down, row2(ln2_g[l]), row2(ln2_b[l]), alpha=alpha)
    return x
```

```python
import functools
import math

import jax
import jax.numpy as jnp
from jax import lax
from jax.experimental import pallas as pl
from jax.experimental.pallas import tpu as pltpu

F32 = jnp.float32
BF16 = jnp.bfloat16

CHUNK = 128
SGU_GROUPS = 4
DIFF_HEADS = 4
HEAD_DIM = 64
ROPE_DIM = HEAD_DIM // 4
ROPE_HALF = ROPE_DIM // 2
ROPE_THETA = 500000.0
CONV_K = 3
EPS = 1e-5
LOG2E = 1.4426950408889634
NEG_BIG = -1e30
SUM_ROWS = 16

PROJ_TILE = 1024
PROJ_ROWS = 256
ATTN_Q_BLOCK = 512
ATTN_K_BLOCK = 256
FFN_TILE = 1024
FFN_ROWS = 256
VMEM_LIMIT = 56 * 1024 * 1024


def _dot(a, b):
    return jnp.dot(a, b, preferred_element_type=F32)


def _layer_norm(x, g, b):
    mu = jnp.mean(x, axis=-1, keepdims=True)
    xc = x - mu
    var = jnp.mean(xc * xc, axis=-1, keepdims=True)
    return xc * lax.rsqrt(var + EPS) * g + b


def _resident(arr, *lead):
    tail = arr.shape[len(lead):]
    return pl.BlockSpec((None,) * len(lead) + tail, lambda *_: lead + (0,) * len(tail),
                        pipeline_mode=pl.Buffered(1))


def _in_proj_body(x_ref, cos_ref, sin_ref, w_in_ref, wsgu_ref, bsgu_ref, lng_ref, lnb_ref,
                  convw_ref, wb0_ref, wb2_ref,
                  qT_ref, k_ref, vT_ref, part_ref, gb_ref, carry_ref, *, d_model):
    tm = x_ref.shape[0]
    half = d_model // 2
    o1 = d_model
    qk_w = 2 * DIFF_HEADS * HEAD_DIM
    v_w = DIFF_HEADS * 2 * HEAD_DIM
    o3 = o1 + 2 * qk_w
    o4 = o3 + v_w
    o5 = o4 + 3 * half

    def piece(rs, prev):
        rows = rs.stop - rs.start
        xb = x_ref[rs, :].astype(BF16)

        def proj(lo, hi):
            return _dot(xb, w_in_ref[:, lo:hi])

        za = proj(0, d_model)
        zq_t = proj(o1, o1 + qk_w).T
        zk_t = proj(o1 + qk_w, o3).T

        za = 0.5 * za * (1.0 + lax.erf(za * (1.0 / math.sqrt(2.0))))
        u = za[:, :half]
        v = _layer_norm(za[:, half:], lng_ref[...], lnb_ref[...]).astype(BF16)

        cos, sin = cos_ref[:, rs], sin_ref[:, rs]

        def rope_t(z_t):
            pieces = []
            for lo in range(0, qk_w, HEAD_DIM):
                r1, r2 = z_t[lo:lo + ROPE_HALF], z_t[lo + ROPE_HALF:lo + ROPE_DIM]
                pieces += [r1 * cos - r2 * sin, r2 * cos + r1 * sin, z_t[lo + ROPE_DIM:lo + HEAD_DIM]]
            return jnp.concatenate(pieces, axis=0)

        q_t = (rope_t(zq_t) * (HEAD_DIM ** -0.5 * LOG2E)).astype(BF16)
        no_q = jnp.zeros((HEAD_DIM, rows), BF16)
        maps = []
        for lo in range(0, qk_w, 2 * HEAD_DIM):
            maps += [q_t[lo:lo + HEAD_DIM], no_q, no_q, q_t[lo + HEAD_DIM:lo + 2 * HEAD_DIM]]
        qT_ref[:, rs] = jnp.concatenate(maps, axis=0)
        k_rot = rope_t(zk_t).T.astype(BF16)
        for h in range(DIFF_HEADS):
            k_ref[h, rs, :] = k_rot[:, h * 2 * HEAD_DIM:(h + 1) * 2 * HEAD_DIM]
        vT_ref[rs.start // PROJ_ROWS] = proj(o3, o4).T.astype(BF16)
        gb_ref[rs, :] = jax.nn.sigmoid(proj(o5 + d_model, o5 + 2 * d_model)).astype(BF16)

        gw = half // SGU_GROUPS
        mixed = []
        for c in range(rows // CHUNK):
            cols = [_dot(wsgu_ref[g], v[c * CHUNK:(c + 1) * CHUNK, g * gw:(g + 1) * gw])
                    for g in range(SGU_GROUPS)]
            mixed.append(jnp.concatenate(cols, axis=1) + bsgu_ref[...])
        y_a = (u * jnp.concatenate(mixed, axis=0)).astype(BF16)

        zc = proj(o4, o5)
        h = zc[:, half:2 * half] * zc[:, 2 * half:]
        row = lax.broadcasted_iota(jnp.int32, (rows, half), 0)
        h1 = jnp.where(row == 0, prev[7:8, :], pltpu.roll(h, 1, 0))
        h2 = jnp.where(row == 0, prev[6:7, :], jnp.where(row == 1, prev[7:8, :], pltpu.roll(h, 2, 0)))
        conv = h2 * convw_ref[0:1, :] + h1 * convw_ref[1:2, :] + h * convw_ref[2:3, :]
        y_c = (zc[:, :half] * conv).astype(BF16)

        g_a = jax.nn.sigmoid(proj(o5, o5 + d_model))
        g_c = jax.nn.sigmoid(proj(o5 + 2 * d_model, o5 + 3 * d_model))
        part_ref[rs, :] = (g_a * _dot(y_a, wb0_ref[...]) + g_c * _dot(y_c, wb2_ref[...])).astype(BF16)
        return h[rows - 8:, :]

    @pl.when(pl.program_id(1) == 0)
    def _():
        carry_ref[...] = jnp.zeros_like(carry_ref)

    prev = carry_ref[...]
    for lo in range(0, tm, PROJ_ROWS):
        prev = piece(slice(lo, lo + PROJ_ROWS), prev)
    carry_ref[...] = prev


def _in_proj_call(x, rope_cos, rope_sin, layer, w_in, w_sgu, b_sgu, ln_g, ln_b, conv_w, w_branch):
    assert conv_w.shape[0] == CONV_K == 3
    bsz, s_len, d_model = x.shape
    tm = min(PROJ_TILE, s_len)
    half = d_model // 2
    qk_w = 2 * DIFF_HEADS * HEAD_DIM
    v_w = DIFF_HEADS * 2 * HEAD_DIM
    grid = (bsz, s_len // tm)
    tok = lambda w: pl.BlockSpec((None, tm, w), lambda b, t: (b, t, 0))
    tokT = lambda w: pl.BlockSpec((None, w, tm), lambda b, t: (b, 0, t))
    return pl.pallas_call(
        functools.partial(_in_proj_body, d_model=d_model),
        grid=grid,
        in_specs=[tok(d_model), tokT(ROPE_HALF), tokT(ROPE_HALF),
                  _resident(w_in, layer), _resident(w_sgu, layer), _resident(b_sgu),
                  _resident(ln_g), _resident(ln_b), _resident(conv_w),
                  _resident(w_branch, layer, 0), _resident(w_branch, layer, 2)],
        out_specs=[tokT(2 * qk_w),
                   pl.BlockSpec((None, DIFF_HEADS, tm, 2 * HEAD_DIM), lambda b, t: (b, 0, t, 0)),
                   pl.BlockSpec((None, tm // PROJ_ROWS, v_w, PROJ_ROWS), lambda b, t: (b, t, 0, 0)),
                   tok(d_model), tok(d_model)],
        out_shape=[jax.ShapeDtypeStruct((bsz, 2 * qk_w, s_len), BF16),
                   jax.ShapeDtypeStruct((bsz, DIFF_HEADS, s_len, 2 * HEAD_DIM), BF16),
                   jax.ShapeDtypeStruct((bsz, s_len // PROJ_ROWS, v_w, PROJ_ROWS), BF16),
                   jax.ShapeDtypeStruct((bsz, s_len, d_model), BF16),
                   jax.ShapeDtypeStruct((bsz, s_len, d_model), BF16)],
        scratch_shapes=[pltpu.VMEM((8, half), F32)],
        compiler_params=pltpu.CompilerParams(
            dimension_semantics=("arbitrary", "arbitrary"), vmem_limit_bytes=VMEM_LIMIT),
        name="in_proj",
    )(x, rope_cos, rope_sin, w_in, w_sgu, b_sgu, ln_g, ln_b, conv_w, w_branch, w_branch)


def _attn_body(lam_ref, qm_ref, qn_ref, k_ref, vT_ref, subg_ref, o_ref, sa_ref, sb_ref, acc_ref, m_ref,
               mba_ref, mbb_ref, *, out_scale):
    n_maps, bk, bq = sa_ref.shape
    assert bq == 2 * bk and n_maps == 2 * DIFF_HEADS
    half = bq // 2
    i = pl.program_id(1)
    n_off = 2 * i
    lam = lam_ref[0, 0]
    pair_w = 2 * HEAD_DIM
    visible = (lax.broadcasted_iota(jnp.int32, (bk, bq), 0)
               <= lax.broadcasted_iota(jnp.int32, (bk, bq), 1))

    def colmax(s):
        return jnp.max(s, axis=0, keepdims=True)

    def rows(r):
        return slice(r * pair_w, (r + 1) * pair_w)

    def keys(kb):
        return pl.ds(pl.multiple_of(kb * bk, bk), bk)

    def values(h, kb):
        return jnp.concatenate([vT_ref[kb, rows(h), :], jnp.ones((SUM_ROWS, bk), BF16)], axis=0)

    def scores(idx, kb, q_lo=0, q_ref=qm_ref):
        return _dot(k_ref[idx // 2, keys(kb), :], q_ref[rows(idx), q_lo:])

    def stage(s_ref, mb_ref, idx, kb, q_ref=qm_ref):
        s = scores(idx, kb, 0, q_ref)
        s_ref[idx] = s
        mb_ref[idx] = colmax(s)

    def reset(idx):
        acc_ref[idx] = jnp.zeros(acc_ref.shape[1:], F32)
        m_ref[idx] = jnp.full(m_ref.shape[1:], NEG_BIG, F32)

    def update(s, idx, kb, mb, q_lo=0):
        m_run = m_ref[idx, :, q_lo:]
        m_new = jnp.maximum(m_run, mb)
        p = jnp.exp2(s - m_new).astype(BF16)
        acc_ref[idx, :, q_lo:] = (jnp.exp2(m_run - m_new) * acc_ref[idx, :, q_lo:]
                                  + _dot(values(idx // 2, kb), p))
        m_ref[idx, :, q_lo:] = m_new

    @pl.when(i == 0)
    def _():
        for idx in range(n_maps):
            stage(sa_ref, mba_ref, idx, 0)
        for idx in range(n_maps):
            reset(idx)

    def pair(j):
        for idx in range(n_maps):
            stage(sb_ref, mbb_ref, idx, j + 1)
            update(sa_ref[idx], idx, j, mba_ref[idx])
        for idx in range(n_maps):
            stage(sa_ref, mba_ref, idx, j + 2)
            update(sb_ref[idx], idx, j + 1, mbb_ref[idx])

    odd = i % 2

    @pl.when(odd == 1)
    def _():
        pair(0)

    @pl.loop(0, i // 2)
    def _(t):
        j = 2 * odd + 4 * t
        pair(j)
        pair(j + 2)

    tri = visible[:, :half]
    for idx in range(n_maps):
        sb_ref[idx, :, half:] = jnp.where(tri, scores(idx, n_off + 1, half), NEG_BIG)
        s_lo = jnp.where(tri, sa_ref[idx, :, :half], NEG_BIG)
        s = jnp.concatenate([s_lo, sa_ref[idx, :, half:]], axis=1)
        mb = jnp.concatenate([colmax(s_lo), mba_ref[idx, :, half:]], axis=1)
        update(s, idx, n_off, mb)
        stage(sa_ref, mba_ref, idx, 0, qn_ref)
    for h in range(DIFF_HEADS):
        o_maps = []
        for idx in (2 * h, 2 * h + 1):
            s = sb_ref[idx, :, half:]
            update(s, idx, n_off + 1, colmax(s), half)
            o_maps.append((acc_ref[idx, :pair_w, :], 1.0 / acc_ref[idx, pair_w:pair_w + 1, :]))
        (o0, r0), (o1, r1) = o_maps
        o = o0 * r0 - o1 * (lam * r1)
        reset(2 * h)
        reset(2 * h + 1)
        ms = jnp.mean(o * o, axis=0, keepdims=True)
        o_ref[rows(h), :] = o * lax.rsqrt(ms + EPS) * (subg_ref[...] * out_scale)


def _attn_call(lam, qm, k, vT, subg, *, out_scale):
    bsz, _, s_len, _ = k.shape
    bq, bk = ATTN_Q_BLOCK, ATTN_K_BLOCK
    assert bq == 2 * bk and s_len % bq == 0 and vT.shape[3] == bk
    n_maps = 2 * DIFF_HEADS
    n_q = s_len // bq
    grid = (bsz, n_q)
    per_batch = lambda shape: pl.BlockSpec((None,) + shape, lambda b, i: (b, 0, 0, 0))
    return pl.pallas_call(
        functools.partial(_attn_body, out_scale=out_scale),
        grid=grid,
        in_specs=[pl.BlockSpec(memory_space=pltpu.SMEM),
                  pl.BlockSpec((None, qm.shape[1], bq), lambda b, i: (b, 0, i)),
                  pl.BlockSpec((None, qm.shape[1], bq), lambda b, i: (b, 0, jnp.minimum(i + 1, n_q - 1))),
                  per_batch(k.shape[1:]), per_batch(vT.shape[1:]), _resident(subg)],
        out_specs=pl.BlockSpec((None, vT.shape[2], bq), lambda b, i: (b, 0, i)),
        out_shape=jax.ShapeDtypeStruct((bsz, vT.shape[2], s_len), F32),
        scratch_shapes=[pltpu.VMEM((n_maps, bk, bq), F32), pltpu.VMEM((n_maps, bk, bq), F32),
                        pltpu.VMEM((n_maps, 2 * HEAD_DIM + SUM_ROWS, bq), F32),
                        pltpu.VMEM((n_maps, 1, bq), F32), pltpu.VMEM((n_maps, 1, bq), F32),
                        pltpu.VMEM((n_maps, 1, bq), F32)],
        compiler_params=pltpu.CompilerParams(
            dimension_semantics=("arbitrary", "arbitrary"), vmem_limit_bytes=VMEM_LIMIT),
        name="diff_attn",
    )(lam, qm, qm, k, vT, subg)


def _ffn_chunks(d_ff):
    step = 1024
    return [(lo, min(lo + step, d_ff)) for lo in range(0, d_ff, step)]


def _ffn_body(yT_ref, part_ref, gb_ref, x_ref, wb1_ref, wo_ref, g1_ref, b1_ref, wgu_ref, wd_ref,
              g2_ref, b2_ref, o_ref, *, alpha, d_ff):
    tm = x_ref.shape[0]
    pieces = [slice(lo, lo + FFN_ROWS) for lo in range(0, tm, FFN_ROWS)]
    y_b = yT_ref[...].T.astype(BF16)

    branch = [_dot(y_b[rs, :], wb1_ref[...]) for rs in pieces]
    x1 = []
    for rs, br in zip(pieces, branch):
        merged = part_ref[rs, :].astype(F32) + gb_ref[rs, :].astype(F32) * br
        mix = _dot(merged.astype(BF16), wo_ref[...])
        x1.append(_layer_norm(alpha * x_ref[rs, :] + mix, g1_ref[...], b1_ref[...]))
    for rs, x in zip(pieces, x1):
        xb = x.astype(BF16)
        acc = alpha * x
        for lo, hi in _ffn_chunks(d_ff):
            gate = _dot(xb, wgu_ref[:, lo:hi])
            up = _dot(xb, wgu_ref[:, d_ff + lo:d_ff + hi])
            hidden = (gate * jax.nn.sigmoid(gate) * up).astype(BF16)
            acc = acc + _dot(hidden, wd_ref[lo:hi, :])
        o_ref[rs, :] = _layer_norm(acc, g2_ref[...], b2_ref[...])


def _ffn_call(yT, part, gb, x, layer, w_branch, w_o, g1, b1, w_gate_up, w_down, g2, b2, *, alpha):
    bsz, s_len, d_model = x.shape
    d_ff = w_down.shape[1]
    tm = min(FFN_TILE, s_len)
    tok = lambda w: pl.BlockSpec((None, tm, w), lambda b, t: (b, t, 0))
    weights = (w_branch, w_o, g1, b1, w_gate_up, w_down, g2, b2)
    weight_specs = [_resident(w_branch, layer, 1), _resident(w_o, layer), _resident(g1), _resident(b1),
                    _resident(w_gate_up, layer), _resident(w_down, layer), _resident(g2), _resident(b2)]
    return pl.pallas_call(
        functools.partial(_ffn_body, alpha=alpha, d_ff=d_ff),
        grid=(bsz, s_len // tm),
        in_specs=[pl.BlockSpec((None, yT.shape[1], tm), lambda b, t: (b, 0, t)),
                  tok(d_model), tok(d_model), tok(d_model)] + weight_specs,
        out_specs=tok(d_model),
        out_shape=jax.ShapeDtypeStruct((bsz, s_len, d_model), F32),
        compiler_params=pltpu.CompilerParams(
            dimension_semantics=("arbitrary", "arbitrary"), vmem_limit_bytes=VMEM_LIMIT),
        name="merge_swiglu",
    )(yT, part, gb, x, *weights)


def _rope_tables(positions):
    inv_freq = ROPE_THETA ** (-jnp.arange(0, ROPE_DIM, 2, dtype=F32) / ROPE_DIM)
    ang = positions.astype(F32)[:, None, :] * inv_freq[None, :, None]
    return jnp.cos(ang), jnp.sin(ang)


def kernel(x, positions, w_in, w_sgu, b_sgu, sgu_ln_g, sgu_ln_b, lambda_q1, lambda_k1, lambda_q2,
           lambda_k2, subln_g, conv_w, w_branch, w_o, ln1_g, ln1_b, w_gate_up, w_down, ln2_g, ln2_b):
    bsz, s_len, d_model = x.shape
    depth = w_in.shape[0]
    half = d_model // 2
    alpha = (2 * depth) ** 0.25
    rope_cos, rope_sin = _rope_tables(positions)
    tril = jnp.tril(jnp.ones((CHUNK, CHUNK), dtype=bool))
    row2 = lambda a: a.reshape(1, -1)
    w_in, w_branch, w_o, w_gate_up, w_down = (w.astype(BF16) for w in (w_in, w_branch, w_o, w_gate_up, w_down))
    w_sgu = jnp.where(tril, w_sgu, 0.0).astype(BF16)

    for l in range(depth):
        lambda_init = 0.8 - 0.6 * math.exp(-0.3 * l)
        lam = (jnp.exp(jnp.sum(lambda_q1[l] * lambda_k1[l])) - jnp.exp(jnp.sum(lambda_q2[l] * lambda_k2[l]))
               + lambda_init).reshape(1, 1).astype(F32)
        b_s = jnp.repeat(b_sgu[l].T, half // SGU_GROUPS, axis=1)
        qT, k, vT, part, gb = _in_proj_call(
            x, rope_cos, rope_sin, l, w_in, w_sgu, b_s, row2(sgu_ln_g[l]), row2(sgu_ln_b[l]),
            conv_w[l].T, w_branch)
        yT = _attn_call(lam, qT, k, vT, subln_g[l].reshape(-1, 1), out_scale=1.0 - lambda_init)
        x = _ffn_call(yT, part, gb, x, l, w_branch, w_o, row2(ln1_g[l]), row2(ln1_b[l]),
                      w_gate_up, w_down, row2(ln2_g[l]), row2(ln2_b[l]), alpha=alpha)
    return x
```
